```python
import jax, jax.numpy as jnp
from jax import lax
import numpy as np

D_MODEL = 2048
BATCH = 2
SEQ = 4096
DEPTH = 1

HEAD_DIM = 128
N_HEADS = D_MODEL // HEAD_DIM
N_HEADS_NA = N_HEADS // 2
N_HEADS_DIL = N_HEADS - N_HEADS_NA
D_NA = N_HEADS_NA * HEAD_DIM
D_DIL = N_HEADS_DIL * HEAD_DIM
GRID_W = 64
NA_ROWS_MAX = 8
NA_COLS = 16
DIL_PATTERNS = ((128, 1), (512, 4), (2048, 16))
DIL_BLOCK = 128
ROPE_THETA = 500000.0
ROPE_DIM = HEAD_DIM // 4
D_FF = 5632
N_MOD = 9
EPS = 1e-6
NEG = -1e30

kernel_name = 'hybrid_natten_dilated_macaron_block'


def rms_norm(x, g):
    xf = x.astype(jnp.float32)
    y = xf * lax.rsqrt(jnp.mean(xf * xf, axis=-1, keepdims=True) + EPS)
    return (y * g.astype(jnp.float32)).astype(x.dtype)


def modulate(n, shift, scale):
    return n * (1 + scale[:, None, :]) + shift[:, None, :]


def swiglu(x, w_gate, w_up, w_down):
    return (jax.nn.silu(x @ w_gate) * (x @ w_up)) @ w_down


def rope_tables(seq):
    pos = jnp.arange(seq, dtype=jnp.float32)
    inv = jnp.power(ROPE_THETA, -jnp.arange(0, ROPE_DIM, 2, dtype=jnp.float32) / ROPE_DIM)
    ang = pos[:, None] * inv[None, :]
    return jnp.cos(ang)[:, None, :], jnp.sin(ang)[:, None, :]


def partial_rope(x, cos, sin):
    half = ROPE_DIM // 2
    xf = x.astype(jnp.float32)
    x1, x2, rest = xf[..., :half], xf[..., half:ROPE_DIM], xf[..., ROPE_DIM:]
    out = jnp.concatenate([x1 * cos - x2 * sin, x2 * cos + x1 * sin, rest], axis=-1)
    return out.astype(x.dtype)


def neighbourhood_attention(q, k, v, rpb):
    B, S, H, D = q.shape
    rows = S // GRID_W
    kr = min(NA_ROWS_MAX, rows)
    qg = q.reshape(B, rows, GRID_W, H, D)
    kg = k.reshape(B, rows, GRID_W, H, D)
    vg = v.reshape(B, rows, GRID_W, H, D)
    col = jnp.arange(GRID_W)
    col_start = jnp.clip(col - NA_COLS // 2, 0, GRID_W - NA_COLS)
    col_idx = col_start[:, None] + jnp.arange(NA_COLS)[None, :]
    col_off = col_idx - col[:, None] + (NA_COLS - 1)
    scale = D ** -0.5

    def row_block(r):
        r_start = jnp.clip(r - kr // 2, 0, rows - kr)
        k_rows = lax.dynamic_slice_in_dim(kg, r_start, kr, axis=1)
        v_rows = lax.dynamic_slice_in_dim(vg, r_start, kr, axis=1)
        k_nb = k_rows[:, :, col_idx]
        v_nb = v_rows[:, :, col_idx]
        row_off = r_start + jnp.arange(kr) - r + (NA_ROWS_MAX - 1)
        bias = rpb[:, row_off][:, :, col_off]
        bias = bias.transpose(0, 2, 1, 3)
        q_r = lax.dynamic_index_in_dim(qg, r, axis=1, keepdims=False)
        s = jnp.einsum('bqhd,bkqnhd->bhqkn', q_r, k_nb,
                       preferred_element_type=jnp.float32) * scale
        s = s + bias[None].astype(jnp.float32)
        p = jax.nn.softmax(s.reshape(B, H, GRID_W, kr * NA_COLS), axis=-1)
        p = p.reshape(B, H, GRID_W, kr, NA_COLS).astype(v.dtype)
        return jnp.einsum('bhqkn,bkqnhd->bqhd', p, v_nb)

    out = lax.map(row_block, jnp.arange(rows))
    return out.transpose(1, 0, 2, 3, 4).reshape(B, S, H, D)


def dilated_attention(q, k, v):
    B, S, H, D = q.shape
    nb = S // DIL_BLOCK
    scale = D ** -0.5
    qpos = jnp.arange(DIL_BLOCK)

    def q_block(i):
        t = i * DIL_BLOCK + qpos
        q_b = lax.dynamic_slice_in_dim(q, i * DIL_BLOCK, DIL_BLOCK, axis=1)
        outs, lses = [], []
        for window, dil in DIL_PATTERNS:
            half = window // 2 // dil
            offs = dil * jnp.arange(-half, half + 1)
            idx = t[:, None] + offs[None, :]
            valid = (idx >= 0) & (idx < S)
            idx_c = jnp.clip(idx, 0, S - 1).reshape(-1)
            k_g = jnp.take(k, idx_c, axis=1).reshape(B, DIL_BLOCK, -1, H, D)
            v_g = jnp.take(v, idx_c, axis=1).reshape(B, DIL_BLOCK, -1, H, D)
            s = jnp.einsum('bqhd,bqkhd->bhqk', q_b, k_g,
                           preferred_element_type=jnp.float32) * scale
            s = jnp.where(valid[None, None], s, NEG)
            m = jnp.max(s, axis=-1, keepdims=True)
            e = jnp.exp(s - m)
            den = jnp.sum(e, axis=-1, keepdims=True)
            o = jnp.einsum('bhqk,bqkhd->bqhd', (e / den).astype(v.dtype), v_g)
            outs.append(o)
            lses.append((m + jnp.log(den))[..., 0])
        w = jax.nn.softmax(jnp.stack(lses, axis=0), axis=0)
        w = w.transpose(0, 1, 3, 2).astype(v.dtype)
        return jnp.einsum('pbqh,pbqhd->bqhd', w, jnp.stack(outs, axis=0))

    out = lax.map(q_block, jnp.arange(nb))
    return out.transpose(1, 0, 2, 3, 4).reshape(B, S, H, D)


def setup_inputs(seed: int = 0) -> dict:
    key = jax.random.key(seed)
    ks = jax.random.split(key, 24)
    f32 = jnp.float32
    n = lambda k, shape, s: jax.random.normal(k, shape, f32) * s
    gain = lambda k, shape: 1.0 + 0.02 * jax.random.normal(k, shape, f32)
    L, D = DEPTH, D_MODEL
    return {
        'x': n(ks[0], (BATCH, SEQ, D), 1.0),
        'c': n(ks[1], (BATCH, D), 1.0),
        'w_ada': n(ks[2], (L, D, N_MOD * D), 0.5 * D ** -0.5),
        'b_ada': n(ks[3], (L, N_MOD * D), 0.02),
        'g_ffn1': gain(ks[4], (L, D)),
        'w1_gate': n(ks[5], (L, D, D_FF), D ** -0.5),
        'w1_up': n(ks[6], (L, D, D_FF), D ** -0.5),
        'w1_down': n(ks[7], (L, D_FF, D), D_FF ** -0.5),
        'g_mix': gain(ks[8], (L, D)),
        'w_qkv': n(ks[9], (L, D, 3 * D), D ** -0.5),
        'qn_na': gain(ks[10], (L, HEAD_DIM)),
        'kn_na': gain(ks[11], (L, HEAD_DIM)),
        'qn_dil': gain(ks[12], (L, HEAD_DIM)),
        'kn_dil': gain(ks[13], (L, HEAD_DIM)),
        'rpb_na': n(ks[14], (L, N_HEADS_NA, 2 * NA_ROWS_MAX - 1, 2 * NA_COLS - 1), 0.1),
        'g_out_na': gain(ks[15], (L, D_NA)),
        'g_out_dil': gain(ks[16], (L, D_DIL)),
        'w_o': n(ks[17], (L, D, D), D ** -0.5),
        'g_ffn2': gain(ks[18], (L, D)),
        'w2_gate': n(ks[19], (L, D, D_FF), D ** -0.5),
        'w2_up': n(ks[20], (L, D, D_FF), D ** -0.5),
        'w2_down': n(ks[21], (L, D_FF, D), D_FF ** -0.5),
    }


def reference(x, c, w_ada, b_ada, g_ffn1, w1_gate, w1_up, w1_down, g_mix, w_qkv,
              qn_na, kn_na, qn_dil, kn_dil, rpb_na, g_out_na, g_out_dil, w_o,
              g_ffn2, w2_gate, w2_up, w2_down):
    B, S, D = x.shape
    cos, sin = rope_tables(S)
    h = x
    for l in range(DEPTH):
        mod = jax.nn.silu(c) @ w_ada[l] + b_ada[l]
        sh1, sc1, gt1, sh2, sc2, gt2, sh3, sc3, gt3 = jnp.split(mod, N_MOD, axis=-1)

        f = swiglu(modulate(rms_norm(h, g_ffn1[l]), sh1, sc1), w1_gate[l], w1_up[l], w1_down[l])
        h = h + 0.5 * gt1[:, None, :] * f

        nrm = modulate(rms_norm(h, g_mix[l]), sh2, sc2)
        qkv = nrm @ w_qkv[l]
        q, k, v = jnp.split(qkv, 3, axis=-1)
        q = q.reshape(B, S, N_HEADS, HEAD_DIM)
        k = k.reshape(B, S, N_HEADS, HEAD_DIM)
        v = v.reshape(B, S, N_HEADS, HEAD_DIM)

        qa = rms_norm(q[:, :, :N_HEADS_NA], qn_na[l])
        ka = rms_norm(k[:, :, :N_HEADS_NA], kn_na[l])
        o_na = neighbourhood_attention(qa, ka, v[:, :, :N_HEADS_NA], rpb_na[l])

        qb = partial_rope(rms_norm(q[:, :, N_HEADS_NA:], qn_dil[l]), cos, sin)
        kb = partial_rope(rms_norm(k[:, :, N_HEADS_NA:], kn_dil[l]), cos, sin)
        o_dil = dilated_attention(qb, kb, v[:, :, N_HEADS_NA:])

        o_na = rms_norm(o_na.reshape(B, S, D_NA), g_out_na[l])
        o_dil = rms_norm(o_dil.reshape(B, S, D_DIL), g_out_dil[l])
        mix = jnp.concatenate([o_na, o_dil], axis=-1) @ w_o[l]
        h = h + gt2[:, None, :] * mix

        f = swiglu(modulate(rms_norm(h, g_ffn2[l]), sh3, sc3), w2_gate[l], w2_up[l], w2_down[l])
        h = h + 0.5 * gt3[:, None, :] * f
    return h
```

```python
import functools
import math

import numpy as np
import jax
import jax.numpy as jnp
from jax import lax
from jax.experimental import pallas as pl
from jax.experimental.pallas import tpu as pltpu

HEAD_DIM = 128
N_HEADS = 16
N_HEADS_NA = 8
N_HEADS_DIL = N_HEADS - N_HEADS_NA
GRID_W = 64
NA_ROWS = 8
NA_COLS = 16
DIL_PATTERNS = ((128, 1), (512, 4), (2048, 16))
ROPE_THETA = 500000.0
ROPE_DIM = HEAD_DIM // 4
N_MOD = 9
EPS = 1e-6
NEG = -1e30
ATTN_SCALE = HEAD_DIM ** -0.5

BF16 = jnp.bfloat16
F32 = jnp.float32

MIB = 1024 * 1024

ADA_TN = 512
FFN_TM = 512
FFN_TF = 512
QKV_TM = 512
QKV_TN = 512
OPROJ_TM = 512
NA_RB = 4
NA_KR = 12
DIL_TQ = 256
DIL_SPAN = 1024
DIL_WIN = DIL_TQ + 2 * DIL_SPAN


def _params(sem, vmem_mib):
    return pltpu.CompilerParams(dimension_semantics=sem, vmem_limit_bytes=vmem_mib * MIB)


def _silu(x):
    return x * (1.0 / (1.0 + jnp.exp(-x)))


def _rms(x, gain):
    ms = jnp.mean(x * x, axis=-1, keepdims=True)
    return x * lax.rsqrt(ms + EPS) * gain


def _ada_kernel(c_ref, w_ref, b_ref, o_ref):
    a = _silu(c_ref[...]).astype(BF16)
    w = w_ref[...].astype(BF16)
    o_ref[...] = jnp.dot(a, w, preferred_element_type=F32) + b_ref[...]


def _ada(c_pad, w_ada, b_ada):
    rows, d = c_pad.shape
    n = w_ada.shape[1]
    return pl.pallas_call(
        _ada_kernel,
        grid=(n // ADA_TN,),
        in_specs=[
            pl.BlockSpec((rows, d), lambda j: (0, 0)),
            pl.BlockSpec((d, ADA_TN), lambda j: (0, j)),
            pl.BlockSpec((1, ADA_TN), lambda j: (0, j)),
        ],
        out_specs=pl.BlockSpec((rows, ADA_TN), lambda j: (0, j)),
        out_shape=jax.ShapeDtypeStruct((rows, n), F32),
        compiler_params=_params(("arbitrary",), 24),
        name="ada_mod",
    )(c_pad, w_ada, b_ada.reshape(1, n))


def _ffn_kernel(x_ref, g_ref, sh_ref, sc_ref, gt_ref, wg_ref, wu_ref, wd_ref, o_ref, xn_ref):
    j = pl.program_id(2)

    @pl.when(j == 0)
    def _():
        y = _rms(x_ref[0], g_ref[...])
        xn_ref[...] = (y * (1.0 + sc_ref[0]) + sh_ref[0]).astype(BF16)
        o_ref[0] = jnp.zeros(o_ref.shape[1:], F32)

    xn = xn_ref[...]
    g = jnp.dot(xn, wg_ref[...], preferred_element_type=F32)
    u = jnp.dot(xn, wu_ref[...], preferred_element_type=F32)
    a = (_silu(g) * u).astype(BF16)
    o_ref[0] += jnp.dot(a, wd_ref[...], preferred_element_type=F32)

    @pl.when(j == pl.num_programs(2) - 1)
    def _():
        o_ref[0] = x_ref[0] + (0.5 * gt_ref[0]) * o_ref[0]


def _ffn(h, gain, shift, scale, gate, w_gate, w_up, w_down):
    b, s, d = h.shape
    dff = w_gate.shape[1]
    vec = pl.BlockSpec((1, 1, d), lambda bi, i, j: (bi, 0, 0))
    return pl.pallas_call(
        _ffn_kernel,
        grid=(b, s // FFN_TM, dff // FFN_TF),
        in_specs=[
            pl.BlockSpec((1, FFN_TM, d), lambda bi, i, j: (bi, i, 0)),
            pl.BlockSpec((1, d), lambda bi, i, j: (0, 0)),
            vec, vec, vec,
            pl.BlockSpec((d, FFN_TF), lambda bi, i, j: (0, j)),
            pl.BlockSpec((d, FFN_TF), lambda bi, i, j: (0, j)),
            pl.BlockSpec((FFN_TF, d), lambda bi, i, j: (j, 0)),
        ],
        out_specs=pl.BlockSpec((1, FFN_TM, d), lambda bi, i, j: (bi, i, 0)),
        out_shape=jax.ShapeDtypeStruct((b, s, d), F32),
        scratch_shapes=[pltpu.VMEM((FFN_TM, d), BF16)],
        compiler_params=_params(("parallel", "parallel", "arbitrary"), 48),
        name="ffn",
    )(h, gain.reshape(1, d), shift, scale, gate, w_gate, w_up, w_down)


def _qkv_kernel(x_ref, g_ref, sh_ref, sc_ref, w_ref, hg_ref, cos_ref, sa_ref, sb_ref,
                o_ref, xn_ref):
    j = pl.program_id(2)
    heads_per_step = QKV_TN // HEAD_DIM
    steps_per_group = N_HEADS_NA // heads_per_step
    n_qk_steps = 4 * steps_per_group

    @pl.when(j == 0)
    def _():
        y = _rms(x_ref[0], g_ref[...])
        xn_ref[...] = (y * (1.0 + sc_ref[0]) + sh_ref[0]).astype(BF16)

    r = jnp.dot(xn_ref[...], w_ref[...], preferred_element_type=F32)
    is_qk = j < n_qk_steps
    is_dil = (j // steps_per_group) % 2 == 1

    @pl.when(jnp.logical_not(is_qk))
    def _():
        o_ref[0] = r.astype(BF16)

    @pl.when(jnp.logical_and(is_qk, jnp.logical_not(is_dil)))
    def _():
        for hh in range(heads_per_step):
            sl = slice(hh * HEAD_DIM, (hh + 1) * HEAD_DIM)
            o_ref[0, :, sl] = _rms(r[:, sl], hg_ref[0]).astype(BF16)

    @pl.when(jnp.logical_and(is_qk, is_dil))
    def _():
        cos, sa, sb = cos_ref[...], sa_ref[...], sb_ref[...]
        half = ROPE_DIM // 2
        for hh in range(heads_per_step):
            sl = slice(hh * HEAD_DIM, (hh + 1) * HEAD_DIM)
            y = _rms(r[:, sl], hg_ref[0])
            y = (y * cos + pltpu.roll(y, half, 1) * sa
                 + pltpu.roll(y, HEAD_DIM - half, 1) * sb)
            o_ref[0, :, sl] = y.astype(BF16)


def _qkv(h, gain, shift, scale, w_qkv, head_gains, rope_c, rope_a, rope_b):
    b, s, d = h.shape
    n = w_qkv.shape[1]
    vec = pl.BlockSpec((1, 1, d), lambda bi, i, j: (bi, 0, 0))
    rope = pl.BlockSpec((QKV_TM, HEAD_DIM), lambda bi, i, j: (i, 0))
    return pl.pallas_call(
        _qkv_kernel,
        grid=(b, s // QKV_TM, n // QKV_TN),
        in_specs=[
            pl.BlockSpec((1, QKV_TM, d), lambda bi, i, j: (bi, i, 0)),
            pl.BlockSpec((1, d), lambda bi, i, j: (0, 0)),
            vec, vec,
            pl.BlockSpec((d, QKV_TN), lambda bi, i, j: (0, j)),
            pl.BlockSpec((1, 1, HEAD_DIM), lambda bi, i, j: (j, 0, 0)),
            rope, rope, rope,
        ],
        out_specs=pl.BlockSpec((1, QKV_TM, QKV_TN), lambda bi, i, j: (bi, i, j)),
        out_shape=jax.ShapeDtypeStruct((b, s, n), BF16),
        scratch_shapes=[pltpu.VMEM((QKV_TM, d), BF16)],
        compiler_params=_params(("parallel", "parallel", "arbitrary"), 40),
        name="qkv",
    )(h, gain.reshape(1, d), shift, scale, w_qkv, head_gains, rope_c, rope_a, rope_b)


def _rope_tables(s):
    pos = jnp.arange(s, dtype=F32)
    inv = jnp.power(ROPE_THETA, -jnp.arange(0, ROPE_DIM, 2, dtype=F32) / ROPE_DIM)
    ang = pos[:, None] * inv[None, :]
    cos, sin = jnp.cos(ang), jnp.sin(ang)
    half = ROPE_DIM // 2
    rest = HEAD_DIM - ROPE_DIM
    rope_c = jnp.concatenate([cos, cos, jnp.ones((s, rest), F32)], axis=1)
    rope_a = jnp.concatenate([jnp.zeros((s, half), F32), sin, jnp.zeros((s, rest), F32)], axis=1)
    rope_b = jnp.concatenate([-sin, jnp.zeros((s, half + rest), F32)], axis=1)
    return rope_c, rope_a, rope_b


def _na_window_start(ib, rows):
    return jnp.clip(NA_RB * ib - NA_ROWS // 2, 0, rows - NA_KR)


def _na_bias_kernel(rpb_ref, o_ref, *, rows):
    h = pl.program_id(0)
    n_ro = 2 * NA_ROWS - 1
    n_co = 2 * NA_COLS - 1
    c = lax.broadcasted_iota(jnp.int32, (GRID_W, 2 * GRID_W), 0)
    lane = lax.broadcasted_iota(jnp.int32, (GRID_W, 2 * GRID_W), 1)
    kc = lane & (GRID_W - 1)
    diff = kc - c + (NA_COLS - 1)
    cs = jnp.clip(c - NA_COLS // 2, 0, GRID_W - NA_COLS)
    col_ok = jnp.logical_and(kc >= cs, kc < cs + NA_COLS)
    neg = jnp.full((GRID_W, 2 * GRID_W), NEG, F32)
    toeplitz = []
    for ro in range(n_ro):
        t = jnp.zeros((GRID_W, 2 * GRID_W), F32)
        for m in range(n_co):
            t = jnp.where(diff == m, rpb_ref[(h * n_ro + ro) * n_co + m], t)
        toeplitz.append(jnp.where(col_ok, t, neg))
    low_half = lane < GRID_W
    n_blocks = rows // NA_RB
    for var, ib in enumerate((0, n_blocks // 2, n_blocks - 1)):
        ws = min(max(NA_RB * ib - NA_ROWS // 2, 0), rows - NA_KR)
        for qr in range(NA_RB):
            r = NA_RB * ib + qr
            r_start = min(max(r - NA_ROWS // 2, 0), rows - NA_ROWS)
            for g in range(NA_KR // 2):
                pieces = []
                for kr in (2 * g, 2 * g + 1):
                    ka = ws + kr
                    inside = r_start <= ka < r_start + NA_ROWS
                    pieces.append(toeplitz[ka - r + NA_ROWS - 1] if inside else neg)
                o_ref[0, var, qr * GRID_W:(qr + 1) * GRID_W, g * 2 * GRID_W:(g + 1) * 2 * GRID_W] = (
                    jnp.where(low_half, pieces[0], pieces[1]))


def _na_bias(rpb, rows):
    nh = rpb.shape[0]
    return pl.pallas_call(
        functools.partial(_na_bias_kernel, rows=rows),
        grid=(nh,),
        in_specs=[pl.BlockSpec(memory_space=pltpu.SMEM)],
        out_specs=pl.BlockSpec((1, 3, NA_RB * GRID_W, NA_KR * GRID_W), lambda h: (h, 0, 0, 0)),
        out_shape=jax.ShapeDtypeStruct((nh, 3, NA_RB * GRID_W, NA_KR * GRID_W), F32),
        compiler_params=_params(("arbitrary",), 24),
        name="na_bias",
    )(rpb.reshape(-1))


def _softmax_pv(s, v):
    m = jnp.max(s, axis=-1, keepdims=True)
    p = jnp.exp(s - m)
    l = jnp.sum(p, axis=-1, keepdims=True)
    return jnp.dot(p.astype(BF16), v, preferred_element_type=F32) / l


def _na_kernel(q_ref, k_ref, v_ref, bias_ref, o_ref, *, rows):
    ib = pl.program_id(2)
    start = pl.multiple_of(_na_window_start(ib, rows) * GRID_W, GRID_W)
    k = k_ref[0, pl.ds(start, NA_KR * GRID_W), :]
    v = v_ref[0, pl.ds(start, NA_KR * GRID_W), :]
    s = lax.dot_general(q_ref[0], k, (((1,), (1,)), ((), ())), preferred_element_type=F32)
    s = s * ATTN_SCALE + bias_ref[0, 0]
    o_ref[0] = _softmax_pv(s, v)


def _na_attention(qkv, bias, d_model):
    b, s, _ = qkv.shape
    rows = s // GRID_W
    n_blocks = rows // NA_RB
    heads_total = d_model // HEAD_DIM
    tq = NA_RB * GRID_W

    def variant(ib):
        return jnp.where(ib == 0, 0, jnp.where(ib == n_blocks - 1, 2, 1))

    return pl.pallas_call(
        functools.partial(_na_kernel, rows=rows),
        grid=(b, N_HEADS_NA, n_blocks),
        in_specs=[
            pl.BlockSpec((1, tq, HEAD_DIM), lambda bi, h, ib: (bi, ib, h)),
            pl.BlockSpec((1, s, HEAD_DIM), lambda bi, h, ib: (bi, 0, heads_total + h)),
            pl.BlockSpec((1, s, HEAD_DIM), lambda bi, h, ib: (bi, 0, 2 * heads_total + h)),
            pl.BlockSpec((1, 1, tq, NA_KR * GRID_W), lambda bi, h, ib: (h, variant(ib), 0, 0)),
        ],
        out_specs=pl.BlockSpec((1, tq, HEAD_DIM), lambda bi, h, ib: (bi, ib, h)),
        out_shape=jax.ShapeDtypeStruct((b, s, N_HEADS_NA * HEAD_DIM), F32),
        compiler_params=_params(("parallel", "parallel", "arbitrary"), 32),
        name="na_attn",
    )(qkv, qkv, qkv, bias)


def _dil_bias_table():
    q = np.arange(DIL_TQ)[:, None]
    x = np.arange(DIL_WIN + 2 * DIL_SPAN)[None, :]
    d = x - 2 * DIL_SPAN - q
    mult = np.zeros(d.shape, np.int64)
    for window, dil in DIL_PATTERNS:
        reach = (window // 2 // dil) * dil
        mult += ((d % dil == 0) & (np.abs(d) <= reach)).astype(np.int64)
    table = np.where(mult > 0, np.log(np.maximum(mult, 1)), NEG)
    return jnp.asarray(table, F32)


def _dil_kernel(q_ref, k_ref, v_ref, t_ref, o_ref, *, seq):
    ib = pl.program_id(2)
    t0 = ib * DIL_TQ
    ws = jnp.clip(t0 - DIL_SPAN, 0, seq - DIL_WIN)
    off = pl.multiple_of(ws - t0 + 2 * DIL_SPAN, DIL_TQ)
    ws = pl.multiple_of(ws, DIL_TQ)
    k = k_ref[0, pl.ds(ws, DIL_WIN), :]
    v = v_ref[0, pl.ds(ws, DIL_WIN), :]
    s = lax.dot_general(q_ref[0], k, (((1,), (1,)), ((), ())), preferred_element_type=F32)
    s = s * ATTN_SCALE + t_ref[:, pl.ds(off, DIL_WIN)]
    o_ref[0] = _softmax_pv(s, v)


def _dil_attention(qkv, table, d_model):
    b, s, _ = qkv.shape
    heads_total = d_model // HEAD_DIM
    h0 = N_HEADS_NA
    return pl.pallas_call(
        functools.partial(_dil_kernel, seq=s),
        grid=(b, N_HEADS_DIL, s // DIL_TQ),
        in_specs=[
            pl.BlockSpec((1, DIL_TQ, HEAD_DIM), lambda bi, h, ib: (bi, ib, h0 + h)),
            pl.BlockSpec((1, s, HEAD_DIM), lambda bi, h, ib: (bi, 0, heads_total + h0 + h)),
            pl.BlockSpec((1, s, HEAD_DIM), lambda bi, h, ib: (bi, 0, 2 * heads_total + h0 + h)),
            pl.BlockSpec(table.shape, lambda bi, h, ib: (0, 0)),
        ],
        out_specs=pl.BlockSpec((1, DIL_TQ, HEAD_DIM), lambda bi, h, ib: (bi, ib, h)),
        out_shape=jax.ShapeDtypeStruct((b, s, N_HEADS_DIL * HEAD_DIM), F32),
        compiler_params=_params(("parallel", "parallel", "arbitrary"), 48),
        name="dil_attn",
    )(qkv, qkv, qkv, table)


def _oproj_kernel(ona_ref, odil_ref, h_ref, gna_ref, gdil_ref, gt_ref, w_ref, o_ref):
    d_na = ona_ref.shape[2]
    na = _rms(ona_ref[0], gna_ref[...]).astype(BF16)
    nd = _rms(odil_ref[0], gdil_ref[...]).astype(BF16)
    mix = jnp.dot(na, w_ref[:d_na, :], preferred_element_type=F32)
    mix = mix + jnp.dot(nd, w_ref[d_na:, :], preferred_element_type=F32)
    o_ref[0] = h_ref[0] + gt_ref[0] * mix


def _oproj(o_na, o_dil, h, g_na, g_dil, gate, w_o):
    b, s, d = h.shape
    d_na, d_dil = o_na.shape[2], o_dil.shape[2]
    return pl.pallas_call(
        _oproj_kernel,
        grid=(b, s // OPROJ_TM),
        in_specs=[
            pl.BlockSpec((1, OPROJ_TM, d_na), lambda bi, i: (bi, i, 0)),
            pl.BlockSpec((1, OPROJ_TM, d_dil), lambda bi, i: (bi, i, 0)),
            pl.BlockSpec((1, OPROJ_TM, d), lambda bi, i: (bi, i, 0)),
            pl.BlockSpec((1, d_na), lambda bi, i: (0, 0)),
            pl.BlockSpec((1, d_dil), lambda bi, i: (0, 0)),
            pl.BlockSpec((1, 1, d), lambda bi, i: (bi, 0, 0)),
            pl.BlockSpec((d, d), lambda bi, i: (0, 0)),
        ],
        out_specs=pl.BlockSpec((1, OPROJ_TM, d), lambda bi, i: (bi, i, 0)),
        out_shape=jax.ShapeDtypeStruct((b, s, d), F32),
        compiler_params=_params(("parallel", "parallel"), 48),
        name="oproj",
    )(o_na, o_dil, h, g_na.reshape(1, d_na), g_dil.reshape(1, d_dil), gate, w_o)


def kernel(x, c, w_ada, b_ada, g_ffn1, w1_gate, w1_up, w1_down, g_mix, w_qkv, qn_na, kn_na, qn_dil, kn_dil, rpb_na, g_out_na, g_out_dil, w_o, g_ffn2, w2_gate, w2_up, w2_down):
    b, s, d = x.shape
    depth = w_ada.shape[0]
    rows = s // GRID_W
    rope_c, rope_a, rope_b = _rope_tables(s)
    dil_table = _dil_bias_table()
    c_pad = jnp.pad(c, ((0, 8 - b), (0, 0)))
    steps_per_group = N_HEADS_NA // (QKV_TN // HEAD_DIM)
    ones = jnp.ones((HEAD_DIM,), F32)

    h = x
    for l in range(depth):
        mod = _ada(c_pad, w_ada[l], b_ada[l])[:b]
        sh1, sc1, gt1, sh2, sc2, gt2, sh3, sc3, gt3 = [
            m.reshape(b, 1, d) for m in jnp.split(mod, N_MOD, axis=-1)]

        h = _ffn(h, g_ffn1[l], sh1, sc1, gt1,
                 w1_gate[l].astype(BF16), w1_up[l].astype(BF16), w1_down[l].astype(BF16))

        group_gains = [qn_na[l], qn_dil[l], kn_na[l], kn_dil[l], ones, ones]
        head_gains = jnp.stack([g for g in group_gains for _ in range(steps_per_group)])
        qkv = _qkv(h, g_mix[l], sh2, sc2, w_qkv[l].astype(BF16),
                   head_gains.reshape(-1, 1, HEAD_DIM), rope_c, rope_a, rope_b)

        o_na = _na_attention(qkv, _na_bias(rpb_na[l], rows), d)
        o_dil = _dil_attention(qkv, dil_table, d)
        h = _oproj(o_na, o_dil, h, g_out_na[l], g_out_dil[l], gt2, w_o[l].astype(BF16))

        h = _ffn(h, g_ffn2[l], sh3, sc3, gt3,
                 w2_gate[l].astype(BF16), w2_up[l].astype(BF16), w2_down[l].astype(BF16))
    return h
```

```python
import functools
import math

import numpy as np
import jax
import jax.numpy as jnp
from jax import lax
from jax.experimental import pallas as pl
from jax.experimental.pallas import tpu as pltpu

HEAD_DIM = 128
N_HEADS = 16
N_HEADS_NA = 8
N_HEADS_DIL = N_HEADS - N_HEADS_NA
GRID_W = 64
NA_ROWS = 8
NA_COLS = 16
DIL_PATTERNS = ((128, 1), (512, 4), (2048, 16))
ROPE_THETA = 500000.0
ROPE_DIM = HEAD_DIM // 4
N_MOD = 9
EPS = 1e-6
NEG = -1e30
ATTN_SCALE = HEAD_DIM ** -0.5
LOG2_E = math.log2(math.e)

BF16 = jnp.bfloat16
F32 = jnp.float32

MIB = 1024 * 1024

ADA_TN = 512
NORM_ROWS = 32
NORM_UNROLL = 4
FFN_TM = 512
FFN_TF = 512
QKV_TM = 512
QKV_TN = 2048
QKV_PIECE = 256
OPROJ_TM = 512
ATTN_UNROLL = 2
NA_RB = 4
NA_KR = 12
DIL_TQ = 256
DIL_SPAN = 1024
DIL_WIN = DIL_TQ + 2 * DIL_SPAN


def _params(sem, vmem_mib):
    return pltpu.CompilerParams(dimension_semantics=sem, vmem_limit_bytes=vmem_mib * MIB)


def _silu(x):
    return x * (1.0 / (1.0 + jnp.exp(-x)))


def _rms(x, gain):
    ms = jnp.mean(x * x, axis=-1, keepdims=True)
    return x * lax.rsqrt(ms + EPS) * gain


def _norm_modulate_rows(x_ref, g_ref, sh_ref, sc_ref, xn_ref):
    n_rows = xn_ref.shape[0]
    gain = g_ref[...]
    scale1 = 1.0 + sc_ref[0]
    shift = sh_ref[0]

    def body(i, carry):
        rows = pl.ds(pl.multiple_of(i * NORM_ROWS, NORM_ROWS), NORM_ROWS)
        xn_ref[rows, :] = (_rms(x_ref[0, rows, :], gain) * scale1 + shift).astype(BF16)
        return carry

    lax.fori_loop(0, n_rows // NORM_ROWS, body, 0, unroll=NORM_UNROLL)


def _ada_kernel(c_ref, w_ref, b_ref, o_ref):
    a = _silu(c_ref[...]).astype(BF16)
    w = w_ref[...].astype(BF16)
    o_ref[...] = jnp.dot(a, w, preferred_element_type=F32) + b_ref[...]


def _ada(c_pad, w_ada, b_ada):
    rows, d = c_pad.shape
    n = w_ada.shape[1]
    return pl.pallas_call(
        _ada_kernel,
        grid=(n // ADA_TN,),
        in_specs=[
            pl.BlockSpec((rows, d), lambda j: (0, 0)),
            pl.BlockSpec((d, ADA_TN), lambda j: (0, j)),
            pl.BlockSpec((1, ADA_TN), lambda j: (0, j)),
        ],
        out_specs=pl.BlockSpec((rows, ADA_TN), lambda j: (0, j)),
        out_shape=jax.ShapeDtypeStruct((rows, n), F32),
        compiler_params=_params(("arbitrary",), 24),
        name="ada_mod",
    )(c_pad, w_ada, b_ada.reshape(1, n))


def _ffn_kernel(x_ref, g_ref, sh_ref, sc_ref, gt_ref, wg_ref, wu_ref, wd_ref, o_ref, xn_ref):
    j = pl.program_id(2)

    @pl.when(j == 0)
    def _():
        _norm_modulate_rows(x_ref, g_ref, sh_ref, sc_ref, xn_ref)
        o_ref[0] = jnp.zeros(o_ref.shape[1:], F32)

    xn = xn_ref[...]
    g = jnp.dot(xn, wg_ref[...], preferred_element_type=F32)
    u = jnp.dot(xn, wu_ref[...], preferred_element_type=F32)
    a = (_silu(g) * u).astype(BF16)
    o_ref[0] += jnp.dot(a, wd_ref[...], preferred_element_type=F32)

    @pl.when(j == pl.num_programs(2) - 1)
    def _():
        o_ref[0] = x_ref[0] + (0.5 * gt_ref[0]) * o_ref[0]


def _ffn(h, gain, shift, scale, gate, w_gate, w_up, w_down):
    b, s, d = h.shape
    dff = w_gate.shape[1]
    vec = pl.BlockSpec((1, 1, d), lambda bi, i, j: (bi, 0, 0))
    return pl.pallas_call(
        _ffn_kernel,
        grid=(b, s // FFN_TM, dff // FFN_TF),
        in_specs=[
            pl.BlockSpec((1, FFN_TM, d), lambda bi, i, j: (bi, i, 0)),
            pl.BlockSpec((1, d), lambda bi, i, j: (0, 0)),
            vec, vec, vec,
            pl.BlockSpec((d, FFN_TF), lambda bi, i, j: (0, j)),
            pl.BlockSpec((d, FFN_TF), lambda bi, i, j: (0, j)),
            pl.BlockSpec((FFN_TF, d), lambda bi, i, j: (j, 0)),
        ],
        out_specs=pl.BlockSpec((1, FFN_TM, d), lambda bi, i, j: (bi, i, 0)),
        out_shape=jax.ShapeDtypeStruct((b, s, d), F32),
        scratch_shapes=[pltpu.VMEM((FFN_TM, d), BF16)],
        compiler_params=_params(("parallel", "parallel", "arbitrary"), 48),
        name="ffn",
    )(h, gain.reshape(1, d), shift, scale, gate, w_gate, w_up, w_down)


def _qkv_kernel(x_ref, g_ref, sh_ref, sc_ref, w_ref, hg_ref, cos_ref, sin_ref, o_ref, xn_ref):
    j = pl.program_id(2)
    n_pieces = QKV_TN // QKV_PIECE
    heads_per_piece = QKV_PIECE // HEAD_DIM

    @pl.when(j == 0)
    def _():
        _norm_modulate_rows(x_ref, g_ref, sh_ref, sc_ref, xn_ref)

    def normed(r, head):
        return _rms(r, hg_ref[0, head // N_HEADS_NA])

    def normed_rotary(r, head):
        y = normed(r, head)
        return y * cos_ref[...] + pltpu.roll(y, HEAD_DIM // 2, 1) * sin_ref[...]

    def pieces(epilogue_of_head):
        xn = xn_ref[...]
        for p in list(range(n_pieces // 2, n_pieces)) + list(range(n_pieces // 2)):
            r = jnp.dot(xn, w_ref[:, p * QKV_PIECE:(p + 1) * QKV_PIECE],
                        preferred_element_type=F32)
            for hh in range(heads_per_piece):
                head = p * heads_per_piece + hh
                dst = slice(head * HEAD_DIM, (head + 1) * HEAD_DIM)
                y = epilogue_of_head(head)(r[:, hh * HEAD_DIM:(hh + 1) * HEAD_DIM], head)
                o_ref[0, :, dst] = y.astype(BF16)

    @pl.when(j < 2)
    def _():
        pieces(lambda head: normed if head < N_HEADS_NA else normed_rotary)

    @pl.when(j == 2)
    def _():
        pieces(lambda head: (lambda r, _: r))


def _qkv(h, gain, shift, scale, w_qkv, head_gains, rope_cos, rope_sin):
    b, s, d = h.shape
    n = w_qkv.shape[1]
    assert QKV_TN == N_HEADS * HEAD_DIM and n == 3 * QKV_TN
    vec = pl.BlockSpec((1, 1, d), lambda bi, i, j: (bi, 0, 0))
    rope = pl.BlockSpec((QKV_TM, HEAD_DIM), lambda bi, i, j: (i, 0))
    return pl.pallas_call(
        _qkv_kernel,
        grid=(b, s // QKV_TM, n // QKV_TN),
        in_specs=[
            pl.BlockSpec((1, QKV_TM, d), lambda bi, i, j: (bi, i, 0)),
            pl.BlockSpec((1, d), lambda bi, i, j: (0, 0)),
            vec, vec,
            pl.BlockSpec((d, QKV_TN), lambda bi, i, j: (0, j)),
            pl.BlockSpec((1, 2, 1, HEAD_DIM), lambda bi, i, j: (jnp.minimum(j, 1), 0, 0, 0)),
            rope, rope,
        ],
        out_specs=pl.BlockSpec((1, QKV_TM, QKV_TN), lambda bi, i, j: (bi, i, j)),
        out_shape=jax.ShapeDtypeStruct((b, s, n), BF16),
        scratch_shapes=[pltpu.VMEM((QKV_TM, d), BF16)],
        compiler_params=_params(("parallel", "parallel", "arbitrary"), 48),
        name="qkv",
    )(h, gain.reshape(1, d), shift, scale, w_qkv, head_gains, rope_cos, rope_sin)


def _rotary_lane_order():
    half = ROPE_DIM // 2
    mid = HEAD_DIM // 2
    order = (list(range(half)) + list(range(ROPE_DIM, ROPE_DIM + mid - half))
             + list(range(half, ROPE_DIM)) + list(range(ROPE_DIM + mid - half, HEAD_DIM)))
    assert sorted(order) == list(range(HEAD_DIM))
    return order


def _take_lanes(a, order):
    runs, start = [], 0
    for i in range(1, len(order) + 1):
        if i == len(order) or order[i] != order[i - 1] + 1:
            runs.append(a[..., order[start]:order[i - 1] + 1])
            start = i
    return jnp.concatenate(runs, axis=-1)


def _rope_tables(s):
    pos = jnp.arange(s, dtype=F32)
    inv = jnp.power(ROPE_THETA, -jnp.arange(0, ROPE_DIM, 2, dtype=F32) / ROPE_DIM)
    ang = pos[:, None] * inv[None, :]
    cos, sin = jnp.cos(ang), jnp.sin(ang)
    half = ROPE_DIM // 2
    gap = HEAD_DIM // 2 - half
    ones, zeros = jnp.ones((s, gap), F32), jnp.zeros((s, gap), F32)
    rope_cos = jnp.concatenate([cos, ones, cos, ones], axis=1)
    rope_sin = jnp.concatenate([-sin, zeros, sin, zeros], axis=1)
    return rope_cos, rope_sin


def _prep_qkv_weight(w_qkv):
    d = w_qkv.shape[0]
    w = w_qkv.reshape(d, 3, N_HEADS, HEAD_DIM)
    order = _rotary_lane_order()
    qk_na = w[:, :2, :N_HEADS_NA]
    qk_dil = _take_lanes(w[:, :2, N_HEADS_NA:], order)
    qk = jnp.concatenate([qk_na, qk_dil], axis=2)
    return jnp.concatenate([qk, w[:, 2:]], axis=1).reshape(d, -1).astype(BF16)


def _na_window_start(ib, rows):
    return jnp.clip(NA_RB * ib - NA_ROWS // 2, 0, rows - NA_KR)


def _na_bias_kernel(rpb_ref, o_ref, *, rows):
    h = pl.program_id(0)
    n_ro = 2 * NA_ROWS - 1
    n_co = 2 * NA_COLS - 1
    c = lax.broadcasted_iota(jnp.int32, (GRID_W, 2 * GRID_W), 0)
    lane = lax.broadcasted_iota(jnp.int32, (GRID_W, 2 * GRID_W), 1)
    kc = lane & (GRID_W - 1)
    diff = kc - c + (NA_COLS - 1)
    cs = jnp.clip(c - NA_COLS // 2, 0, GRID_W - NA_COLS)
    col_ok = jnp.logical_and(kc >= cs, kc < cs + NA_COLS)
    neg = jnp.full((GRID_W, 2 * GRID_W), NEG, F32)
    toeplitz = []
    for ro in range(n_ro):
        t = jnp.zeros((GRID_W, 2 * GRID_W), F32)
        for m in range(n_co):
            t = jnp.where(diff == m, rpb_ref[(h * n_ro + ro) * n_co + m], t)
        toeplitz.append(jnp.where(col_ok, t * LOG2_E, neg))
    low_half = lane < GRID_W
    n_blocks = rows // NA_RB
    for var, ib in enumerate((0, n_blocks // 2, n_blocks - 1)):
        ws = min(max(NA_RB * ib - NA_ROWS // 2, 0), rows - NA_KR)
        for qr in range(NA_RB):
            r = NA_RB * ib + qr
            r_start = min(max(r - NA_ROWS // 2, 0), rows - NA_ROWS)
            for g in range(NA_KR // 2):
                pieces = []
                for kr in (2 * g, 2 * g + 1):
                    ka = ws + kr
                    inside = r_start <= ka < r_start + NA_ROWS
                    pieces.append(toeplitz[ka - r + NA_ROWS - 1] if inside else neg)
                o_ref[0, var, qr * GRID_W:(qr + 1) * GRID_W, g * 2 * GRID_W:(g + 1) * 2 * GRID_W] = (
                    jnp.where(low_half, pieces[0], pieces[1]))


def _na_bias(rpb, rows):
    nh = rpb.shape[0]
    return pl.pallas_call(
        functools.partial(_na_bias_kernel, rows=rows),
        grid=(nh,),
        in_specs=[pl.BlockSpec(memory_space=pltpu.SMEM)],
        out_specs=pl.BlockSpec((1, 3, NA_RB * GRID_W, NA_KR * GRID_W), lambda h: (h, 0, 0, 0)),
        out_shape=jax.ShapeDtypeStruct((nh, 3, NA_RB * GRID_W, NA_KR * GRID_W), F32),
        compiler_params=_params(("arbitrary",), 24),
        name="na_bias",
    )(rpb.reshape(-1))


def _softmax_pv(s2, v):
    m = jnp.max(s2, axis=-1, keepdims=True)
    p = jnp.exp2(s2 - m)
    l = jnp.sum(p, axis=-1, keepdims=True)
    return jnp.dot(p.astype(BF16), v, preferred_element_type=F32) / l


def _na_kernel(q_ref, k_ref, v_ref, bias_ref, o_ref, *, rows):
    n_blocks = rows // NA_RB
    tq = NA_RB * GRID_W

    def block(ib):
        start = pl.multiple_of(_na_window_start(ib, rows) * GRID_W, GRID_W)
        variant = jnp.where(ib == 0, 0, jnp.where(ib == n_blocks - 1, 2, 1))
        qrows = pl.ds(pl.multiple_of(ib * tq, tq), tq)
        k = k_ref[0, pl.ds(start, NA_KR * GRID_W), :]
        v = v_ref[0, pl.ds(start, NA_KR * GRID_W), :]
        s = lax.dot_general(q_ref[0, qrows, :], k, (((1,), (1,)), ((), ())),
                            preferred_element_type=F32)
        o_ref[0, qrows, :] = _softmax_pv(s + bias_ref[0, variant], v)

    def body(i, carry):
        for u in range(ATTN_UNROLL):
            block(i * ATTN_UNROLL + u)
        return carry

    lax.fori_loop(0, n_blocks // ATTN_UNROLL, body, 0)


def _na_attention(qkv, bias, d_model):
    b, s, _ = qkv.shape
    rows = s // GRID_W
    heads_total = d_model // HEAD_DIM
    tq = NA_RB * GRID_W
    return pl.pallas_call(
        functools.partial(_na_kernel, rows=rows),
        grid=(b, N_HEADS_NA),
        in_specs=[
            pl.BlockSpec((1, s, HEAD_DIM), lambda bi, h: (bi, 0, h)),
            pl.BlockSpec((1, s, HEAD_DIM), lambda bi, h: (bi, 0, heads_total + h)),
            pl.BlockSpec((1, s, HEAD_DIM), lambda bi, h: (bi, 0, 2 * heads_total + h)),
            pl.BlockSpec((1, 3, tq, NA_KR * GRID_W), lambda bi, h: (h, 0, 0, 0)),
        ],
        out_specs=pl.BlockSpec((1, s, HEAD_DIM), lambda bi, h: (bi, 0, h)),
        out_shape=jax.ShapeDtypeStruct((b, s, N_HEADS_NA * HEAD_DIM), F32),
        compiler_params=_params(("parallel", "arbitrary"), 32),
        name="na_attn",
    )(qkv, qkv, qkv, bias)


def _dil_bias_table():
    q = np.arange(DIL_TQ)[:, None]
    x = np.arange(DIL_WIN + 2 * DIL_SPAN)[None, :]
    d = x - 2 * DIL_SPAN - q
    mult = np.zeros(d.shape, np.int64)
    for window, dil in DIL_PATTERNS:
        reach = (window // 2 // dil) * dil
        mult += ((d % dil == 0) & (np.abs(d) <= reach)).astype(np.int64)
    table = np.where(mult > 0, np.log2(np.maximum(mult, 1)), NEG)
    return jnp.asarray(table, F32)


def _dil_kernel(q_ref, k_ref, v_ref, t_ref, o_ref, *, seq):
    def block(ib):
        t0 = ib * DIL_TQ
        ws = jnp.clip(t0 - DIL_SPAN, 0, seq - DIL_WIN)
        off = pl.multiple_of(ws - t0 + 2 * DIL_SPAN, DIL_TQ)
        ws = pl.multiple_of(ws, DIL_TQ)
        qrows = pl.ds(pl.multiple_of(t0, DIL_TQ), DIL_TQ)
        k = k_ref[0, pl.ds(ws, DIL_WIN), :]
        v = v_ref[0, pl.ds(ws, DIL_WIN), :]
        s = lax.dot_general(q_ref[0, qrows, :], k, (((1,), (1,)), ((), ())),
                            preferred_element_type=F32)
        o_ref[0, qrows, :] = _softmax_pv(s + t_ref[:, pl.ds(off, DIL_WIN)], v)

    def body(i, carry):
        for u in range(ATTN_UNROLL):
            block(i * ATTN_UNROLL + u)
        return carry

    lax.fori_loop(0, seq // DIL_TQ // ATTN_UNROLL, body, 0)


def _dil_attention(qkv, table, d_model):
    b, s, _ = qkv.shape
    heads_total = d_model // HEAD_DIM
    h0 = N_HEADS_NA
    return pl.pallas_call(
        functools.partial(_dil_kernel, seq=s),
        grid=(b, N_HEADS_DIL),
        in_specs=[
            pl.BlockSpec((1, s, HEAD_DIM), lambda bi, h: (bi, 0, h0 + h)),
            pl.BlockSpec((1, s, HEAD_DIM), lambda bi, h: (bi, 0, heads_total + h0 + h)),
            pl.BlockSpec((1, s, HEAD_DIM), lambda bi, h: (bi, 0, 2 * heads_total + h0 + h)),
            pl.BlockSpec(table.shape, lambda bi, h: (0, 0)),
        ],
        out_specs=pl.BlockSpec((1, s, HEAD_DIM), lambda bi, h: (bi, 0, h)),
        out_shape=jax.ShapeDtypeStruct((b, s, N_HEADS_DIL * HEAD_DIM), F32),
        compiler_params=_params(("parallel", "arbitrary"), 48),
        name="dil_attn",
    )(qkv, qkv, qkv, table)


def _oproj_kernel(ona_ref, odil_ref, h_ref, gna_ref, gdil_ref, gt_ref, w_ref, o_ref):
    d_na = ona_ref.shape[2]
    na = _rms(ona_ref[0], gna_ref[...]).astype(BF16)
    nd = _rms(odil_ref[0], gdil_ref[...]).astype(BF16)
    mix = jnp.dot(na, w_ref[:d_na, :], preferred_element_type=F32)
    mix = mix + jnp.dot(nd, w_ref[d_na:, :], preferred_element_type=F32)
    o_ref[0] = h_ref[0] + gt_ref[0] * mix


def _oproj(o_na, o_dil, h, g_na, g_dil, gate, w_o):
    b, s, d = h.shape
    d_na, d_dil = o_na.shape[2], o_dil.shape[2]
    return pl.pallas_call(
        _oproj_kernel,
        grid=(b, s // OPROJ_TM),
        in_specs=[
            pl.BlockSpec((1, OPROJ_TM, d_na), lambda bi, i: (bi, i, 0)),
            pl.BlockSpec((1, OPROJ_TM, d_dil), lambda bi, i: (bi, i, 0)),
            pl.BlockSpec((1, OPROJ_TM, d), lambda bi, i: (bi, i, 0)),
            pl.BlockSpec((1, d_na), lambda bi, i: (0, 0)),
            pl.BlockSpec((1, d_dil), lambda bi, i: (0, 0)),
            pl.BlockSpec((1, 1, d), lambda bi, i: (bi, 0, 0)),
            pl.BlockSpec((d, d), lambda bi, i: (0, 0)),
        ],
        out_specs=pl.BlockSpec((1, OPROJ_TM, d), lambda bi, i: (bi, i, 0)),
        out_shape=jax.ShapeDtypeStruct((b, s, d), F32),
        compiler_params=_params(("parallel", "parallel"), 48),
        name="oproj",
    )(o_na, o_dil, h, g_na.reshape(1, d_na), g_dil.reshape(1, d_dil), gate, w_o)


def kernel(x, c, w_ada, b_ada, g_ffn1, w1_gate, w1_up, w1_down, g_mix, w_qkv, qn_na, kn_na, qn_dil, kn_dil, rpb_na, g_out_na, g_out_dil, w_o, g_ffn2, w2_gate, w2_up, w2_down):
    b, s, d = x.shape
    depth = w_ada.shape[0]
    rows = s // GRID_W
    rope_cos, rope_sin = _rope_tables(s)
    dil_table = _dil_bias_table()
    c_pad = jnp.pad(c, ((0, 8 - b), (0, 0)))
    lane_order = _rotary_lane_order()
    q_factor = ATTN_SCALE * LOG2_E

    h = x
    for l in range(depth):
        mod = _ada(c_pad, w_ada[l], b_ada[l])[:b]
        sh1, sc1, gt1, sh2, sc2, gt2, sh3, sc3, gt3 = [
            m.reshape(b, 1, d) for m in jnp.split(mod, N_MOD, axis=-1)]

        h = _ffn(h, g_ffn1[l], sh1, sc1, gt1,
                 w1_gate[l].astype(BF16), w1_up[l].astype(BF16), w1_down[l].astype(BF16))

        head_gains = jnp.stack([
            jnp.stack([qn_na[l] * q_factor, _take_lanes(qn_dil[l], lane_order) * q_factor]),
            jnp.stack([kn_na[l], _take_lanes(kn_dil[l], lane_order)]),
        ]).reshape(2, 2, 1, HEAD_DIM)
        qkv = _qkv(h, g_mix[l], sh2, sc2, _prep_qkv_weight(w_qkv[l]), head_gains,
                   rope_cos, rope_sin)

        o_na = _na_attention(qkv, _na_bias(rpb_na[l], rows), d)
        o_dil = _dil_attention(qkv, dil_table, d)
        h = _oproj(o_na, o_dil, h, g_out_na[l], g_out_dil[l], gt2, w_o[l].astype(BF16))

        h = _ffn(h, g_ffn2[l], sh3, sc3, gt3,
                 w2_gate[l].astype(BF16), w2_up[l].astype(BF16), w2_down[l].astype(BF16))
    return h
```

```python
import functools
import math

import numpy as np
import jax
import jax.numpy as jnp
from jax import lax
from jax.experimental import pallas as pl
from jax.experimental.pallas import tpu as pltpu

HEAD_DIM = 128
N_HEADS = 16
N_HEADS_NA = 8
N_HEADS_DIL = N_HEADS - N_HEADS_NA
GRID_W = 64
NA_ROWS = 8
NA_COLS = 16
DIL_PATTERNS = ((128, 1), (512, 4), (2048, 16))
ROPE_THETA = 500000.0
ROPE_DIM = HEAD_DIM // 4
N_MOD = 9
EPS = 1e-6
NEG = -1e30
ATTN_SCALE = HEAD_DIM ** -0.5
LOG2_E = math.log2(math.e)

BF16 = jnp.bfloat16
F32 = jnp.float32

MIB = 1024 * 1024

ADA_TN = 1024
NORM_TM = 512
NORM_ROWS = 32
NORM_UNROLL = 4
FFN_TM = 1024
FFN_TF = 512
FFN_RES = 256
QKV_TM = 512
QKV_TN = 2048
QKV_PIECE = 256
OPROJ_TM = 512
ATTN_UNROLL = 2
NA_RB = 4
NA_KR = 12
DIL_TQ = 256
DIL_SPAN = 1024
DIL_WIN = DIL_TQ + 2 * DIL_SPAN


def _params(sem, vmem_mib):
    return pltpu.CompilerParams(dimension_semantics=sem, vmem_limit_bytes=vmem_mib * MIB)


def _silu(x):
    return x * (1.0 / (1.0 + jnp.exp(-x)))


def _rms(x, gain):
    ms = jnp.mean(x * x, axis=-1, keepdims=True)
    return x * lax.rsqrt(ms + EPS) * gain


def _norm_modulate_rows(x_ref, g_ref, sh_ref, sc_ref, xn_ref):
    n_rows = xn_ref.shape[0]
    gain = g_ref[...]
    scale1 = 1.0 + sc_ref[0]
    shift = sh_ref[0]

    def body(i, carry):
        rows = pl.ds(pl.multiple_of(i * NORM_ROWS, NORM_ROWS), NORM_ROWS)
        xn_ref[rows, :] = (_rms(x_ref[rows, :], gain) * scale1 + shift).astype(BF16)
        return carry

    lax.fori_loop(0, n_rows // NORM_ROWS, body, 0, unroll=NORM_UNROLL)


def _norm_mod_kernel(x_ref, g_ref, sh_ref, sc_ref, xn_ref):
    _norm_modulate_rows(x_ref.at[0], g_ref, sh_ref, sc_ref, xn_ref.at[0])


def _norm_mod(x, gain, shift, scale):
    b, s, d = x.shape
    tile = pl.BlockSpec((1, NORM_TM, d), lambda bi, i: (bi, i, 0))
    vec = pl.BlockSpec((1, 1, d), lambda bi, i: (bi, 0, 0))
    return pl.pallas_call(
        _norm_mod_kernel,
        grid=(b, s // NORM_TM),
        in_specs=[tile, pl.BlockSpec((1, d), lambda bi, i: (0, 0)), vec, vec],
        out_specs=tile,
        out_shape=jax.ShapeDtypeStruct((b, s, d), BF16),
        compiler_params=_params(("parallel", "parallel"), 24),
        name="norm_mod",
    )(x, gain.reshape(1, d), shift, scale)


def _ada_kernel(c_ref, w_ref, b_ref, o_ref):
    a = _silu(c_ref[...]).astype(BF16)
    w = w_ref[...].astype(BF16)
    o_ref[...] = jnp.dot(a, w, preferred_element_type=F32) + b_ref[...]


def _ada(c_pad, w_ada, b_ada):
    rows, d = c_pad.shape
    n = w_ada.shape[1]
    return pl.pallas_call(
        _ada_kernel,
        grid=(n // ADA_TN,),
        in_specs=[
            pl.BlockSpec((rows, d), lambda j: (0, 0)),
            pl.BlockSpec((d, ADA_TN), lambda j: (0, j)),
            pl.BlockSpec((1, ADA_TN), lambda j: (0, j)),
        ],
        out_specs=pl.BlockSpec((rows, ADA_TN), lambda j: (0, j)),
        out_shape=jax.ShapeDtypeStruct((rows, n), F32),
        compiler_params=_params(("arbitrary",), 24),
        name="ada_mod",
    )(c_pad, w_ada, b_ada.reshape(1, n))


def _ffn_kernel(xn_ref, xres_ref, gt_ref, wg_ref, wu_ref, wd_ref, *rest, emit_next):
    if emit_next:
        gn_ref, shn_ref, scn_ref, o_ref, xnn_ref = rest
    else:
        (o_ref,) = rest
    j = pl.program_id(2)
    n_res = o_ref.shape[2] // FFN_RES

    @pl.when(j == 0)
    def _():
        o_ref[0] = jnp.zeros(o_ref.shape[1:], F32)

    xn = xn_ref[0]
    g = jnp.dot(xn, wg_ref[...], preferred_element_type=F32)
    u = jnp.dot(xn, wu_ref[...], preferred_element_type=F32)
    a = (_silu(g) * u).astype(BF16)
    o_ref[0] += (0.5 * gt_ref[0]) * jnp.dot(a, wd_ref[...], preferred_element_type=F32)

    @pl.when(j < n_res)
    def _():
        cols = pl.ds(pl.multiple_of(j * FFN_RES, FFN_RES), FFN_RES)
        o_ref[0, :, cols] += xres_ref[0]

    if emit_next:
        @pl.when(j == pl.num_programs(2) - 1)
        def _():
            _norm_modulate_rows(o_ref.at[0], gn_ref, shn_ref, scn_ref, xnn_ref.at[0])


def _ffn(xn, x, gate, w_gate, w_up, w_down, next_norm=None):
    b, s, d = x.shape
    dff = w_gate.shape[1]
    n_steps = dff // FFN_TF
    n_res = d // FFN_RES
    assert n_res <= n_steps
    tile = pl.BlockSpec((1, FFN_TM, d), lambda bi, i, j: (bi, i, 0))
    vec = pl.BlockSpec((1, 1, d), lambda bi, i, j: (bi, 0, 0))
    in_specs = [
        tile,
        pl.BlockSpec((1, FFN_TM, FFN_RES), lambda bi, i, j: (bi, i, jnp.minimum(j, n_res - 1))),
        vec,
        pl.BlockSpec((d, FFN_TF), lambda bi, i, j: (0, j)),
        pl.BlockSpec((d, FFN_TF), lambda bi, i, j: (0, j)),
        pl.BlockSpec((FFN_TF, d), lambda bi, i, j: (j, 0)),
    ]
    args = [xn, x, gate, w_gate, w_up, w_down]
    out_specs = [tile]
    out_shape = [jax.ShapeDtypeStruct((b, s, d), F32)]
    if next_norm is not None:
        gain, shift, scale = next_norm
        in_specs += [pl.BlockSpec((1, d), lambda bi, i, j: (0, 0)), vec, vec]
        args += [gain.reshape(1, d), shift, scale]
        out_specs.append(tile)
        out_shape.append(jax.ShapeDtypeStruct((b, s, d), BF16))
    outs = pl.pallas_call(
        functools.partial(_ffn_kernel, emit_next=next_norm is not None),
        grid=(b, s // FFN_TM, n_steps),
        in_specs=in_specs,
        out_specs=out_specs,
        out_shape=out_shape,
        compiler_params=_params(("parallel", "parallel", "arbitrary"), 56),
        name="ffn",
    )(*args)
    return outs if next_norm is not None else outs[0]


def _qkv_kernel(xn_ref, w_ref, hg_ref, cos_ref, sin_ref, o_ref):
    j = pl.program_id(2)
    n_pieces = QKV_TN // QKV_PIECE
    heads_per_piece = QKV_PIECE // HEAD_DIM

    def normed(r, head):
        return _rms(r, hg_ref[0, head // N_HEADS_NA])

    def normed_rotary(r, head):
        y = normed(r, head)
        return y * cos_ref[...] + pltpu.roll(y, HEAD_DIM // 2, 1) * sin_ref[...]

    def pieces(epilogue_of_head):
        xn = xn_ref[0]
        for p in list(range(n_pieces // 2, n_pieces)) + list(range(n_pieces // 2)):
            r = jnp.dot(xn, w_ref[:, p * QKV_PIECE:(p + 1) * QKV_PIECE],
                        preferred_element_type=F32)
            for hh in range(heads_per_piece):
                head = p * heads_per_piece + hh
                dst = slice(head * HEAD_DIM, (head + 1) * HEAD_DIM)
                y = epilogue_of_head(head)(r[:, hh * HEAD_DIM:(hh + 1) * HEAD_DIM], head)
                o_ref[0, :, dst] = y.astype(BF16)

    @pl.when(j < 2)
    def _():
        pieces(lambda head: normed if head < N_HEADS_NA else normed_rotary)

    @pl.when(j == 2)
    def _():
        pieces(lambda head: (lambda r, _: r))


def _qkv(xn, w_qkv, head_gains, rope_cos, rope_sin):
    b, s, d = xn.shape
    n = w_qkv.shape[1]
    assert QKV_TN == N_HEADS * HEAD_DIM and n == 3 * QKV_TN
    rope = pl.BlockSpec((QKV_TM, HEAD_DIM), lambda bi, i, j: (i, 0))
    return pl.pallas_call(
        _qkv_kernel,
        grid=(b, s // QKV_TM, n // QKV_TN),
        in_specs=[
            pl.BlockSpec((1, QKV_TM, d), lambda bi, i, j: (bi, i, 0)),
            pl.BlockSpec((d, QKV_TN), lambda bi, i, j: (0, j)),
            pl.BlockSpec((1, 2, 1, HEAD_DIM), lambda bi, i, j: (jnp.minimum(j, 1), 0, 0, 0)),
            rope, rope,
        ],
        out_specs=pl.BlockSpec((1, QKV_TM, QKV_TN), lambda bi, i, j: (bi, i, j)),
        out_shape=jax.ShapeDtypeStruct((b, s, n), BF16),
        compiler_params=_params(("parallel", "parallel", "arbitrary"), 48),
        name="qkv",
    )(xn, w_qkv, head_gains, rope_cos, rope_sin)


def _rotary_lane_order():
    half = ROPE_DIM // 2
    mid = HEAD_DIM // 2
    order = (list(range(half)) + list(range(ROPE_DIM, ROPE_DIM + mid - half))
             + list(range(half, ROPE_DIM)) + list(range(ROPE_DIM + mid - half, HEAD_DIM)))
    assert sorted(order) == list(range(HEAD_DIM))
    return order


def _take_lanes(a, order):
    runs, start = [], 0
    for i in range(1, len(order) + 1):
        if i == len(order) or order[i] != order[i - 1] + 1:
            runs.append(a[..., order[start]:order[i - 1] + 1])
            start = i
    return jnp.concatenate(runs, axis=-1)


def _rope_tables(s):
    pos = jnp.arange(s, dtype=F32)
    inv = jnp.power(ROPE_THETA, -jnp.arange(0, ROPE_DIM, 2, dtype=F32) / ROPE_DIM)
    ang = pos[:, None] * inv[None, :]
    cos, sin = jnp.cos(ang), jnp.sin(ang)
    half = ROPE_DIM // 2
    gap = HEAD_DIM // 2 - half
    ones, zeros = jnp.ones((s, gap), F32), jnp.zeros((s, gap), F32)
    rope_cos = jnp.concatenate([cos, ones, cos, ones], axis=1)
    rope_sin = jnp.concatenate([-sin, zeros, sin, zeros], axis=1)
    return rope_cos, rope_sin


def _prep_qkv_weight(w_qkv):
    d = w_qkv.shape[0]
    w = w_qkv.reshape(d, 3, N_HEADS, HEAD_DIM)
    order = _rotary_lane_order()
    qk_na = w[:, :2, :N_HEADS_NA]
    qk_dil = _take_lanes(w[:, :2, N_HEADS_NA:], order)
    qk = jnp.concatenate([qk_na, qk_dil], axis=2)
    return jnp.concatenate([qk, w[:, 2:]], axis=1).reshape(d, -1).astype(BF16)


def _na_window_start(ib, rows):
    return jnp.clip(NA_RB * ib - NA_ROWS // 2, 0, rows - NA_KR)


def _na_bias_kernel(rpb_ref, o_ref, *, rows):
    h = pl.program_id(0)
    n_ro = 2 * NA_ROWS - 1
    n_co = 2 * NA_COLS - 1
    c = lax.broadcasted_iota(jnp.int32, (GRID_W, 2 * GRID_W), 0)
    lane = lax.broadcasted_iota(jnp.int32, (GRID_W, 2 * GRID_W), 1)
    kc = lane & (GRID_W - 1)
    diff = kc - c + (NA_COLS - 1)
    cs = jnp.clip(c - NA_COLS // 2, 0, GRID_W - NA_COLS)
    col_ok = jnp.logical_and(kc >= cs, kc < cs + NA_COLS)
    neg = jnp.full((GRID_W, 2 * GRID_W), NEG, F32)
    toeplitz = []
    for ro in range(n_ro):
        t = jnp.zeros((GRID_W, 2 * GRID_W), F32)
        for m in range(n_co):
            t = jnp.where(diff == m, rpb_ref[(h * n_ro + ro) * n_co + m], t)
        toeplitz.append(jnp.where(col_ok, t * LOG2_E, neg))
    low_half = lane < GRID_W
    n_blocks = rows // NA_RB
    for var, ib in enumerate((0, n_blocks // 2, n_blocks - 1)):
        ws = min(max(NA_RB * ib - NA_ROWS // 2, 0), rows - NA_KR)
        for qr in range(NA_RB):
            r = NA_RB * ib + qr
            r_start = min(max(r - NA_ROWS // 2, 0), rows - NA_ROWS)
            for g in range(NA_KR // 2):
                pieces = []
                for kr in (2 * g, 2 * g + 1):
                    ka = ws + kr
                    inside = r_start <= ka < r_start + NA_ROWS
                    pieces.append(toeplitz[ka - r + NA_ROWS - 1] if inside else neg)
                o_ref[0, var, qr * GRID_W:(qr + 1) * GRID_W, g * 2 * GRID_W:(g + 1) * 2 * GRID_W] = (
                    jnp.where(low_half, pieces[0], pieces[1]))


def _na_bias(rpb, rows):
    nh = rpb.shape[0]
    return pl.pallas_call(
        functools.partial(_na_bias_kernel, rows=rows),
        grid=(nh,),
        in_specs=[pl.BlockSpec(memory_space=pltpu.SMEM)],
        out_specs=pl.BlockSpec((1, 3, NA_RB * GRID_W, NA_KR * GRID_W), lambda h: (h, 0, 0, 0)),
        out_shape=jax.ShapeDtypeStruct((nh, 3, NA_RB * GRID_W, NA_KR * GRID_W), F32),
        compiler_params=_params(("arbitrary",), 24),
        name="na_bias",
    )(rpb.reshape(-1))


def _softmax_pv(s2, v):
    m = jnp.max(s2, axis=-1, keepdims=True)
    p = jnp.exp2(s2 - m)
    l = jnp.sum(p, axis=-1, keepdims=True)
    return jnp.dot(p.astype(BF16), v, preferred_element_type=F32) / l


def _na_kernel(q_ref, k_ref, v_ref, bias_ref, o_ref, *, rows):
    n_blocks = rows // NA_RB
    tq = NA_RB * GRID_W

    def block(ib):
        start = pl.multiple_of(_na_window_start(ib, rows) * GRID_W, GRID_W)
        variant = jnp.where(ib == 0, 0, jnp.where(ib == n_blocks - 1, 2, 1))
        qrows = pl.ds(pl.multiple_of(ib * tq, tq), tq)
        k = k_ref[0, pl.ds(start, NA_KR * GRID_W), :]
        v = v_ref[0, pl.ds(start, NA_KR * GRID_W), :]
        s = lax.dot_general(q_ref[0, qrows, :], k, (((1,), (1,)), ((), ())),
                            preferred_element_type=F32)
        o_ref[0, qrows, :] = _softmax_pv(s + bias_ref[0, variant], v)

    def body(i, carry):
        for u in range(ATTN_UNROLL):
            block(i * ATTN_UNROLL + u)
        return carry

    lax.fori_loop(0, n_blocks // ATTN_UNROLL, body, 0)


def _na_attention(qkv, bias, d_model):
    b, s, _ = qkv.shape
    rows = s // GRID_W
    heads_total = d_model // HEAD_DIM
    tq = NA_RB * GRID_W
    return pl.pallas_call(
        functools.partial(_na_kernel, rows=rows),
        grid=(b, N_HEADS_NA),
        in_specs=[
            pl.BlockSpec((1, s, HEAD_DIM), lambda bi, h: (bi, 0, h)),
            pl.BlockSpec((1, s, HEAD_DIM), lambda bi, h: (bi, 0, heads_total + h)),
            pl.BlockSpec((1, s, HEAD_DIM), lambda bi, h: (bi, 0, 2 * heads_total + h)),
            pl.BlockSpec((1, 3, tq, NA_KR * GRID_W), lambda bi, h: (h, 0, 0, 0)),
        ],
        out_specs=pl.BlockSpec((1, s, HEAD_DIM), lambda bi, h: (bi, 0, h)),
        out_shape=jax.ShapeDtypeStruct((b, s, N_HEADS_NA * HEAD_DIM), F32),
        compiler_params=_params(("parallel", "arbitrary"), 32),
        name="na_attn",
    )(qkv, qkv, qkv, bias)


def _dil_bias_table():
    q = np.arange(DIL_TQ)[:, None]
    x = np.arange(DIL_WIN + 2 * DIL_SPAN)[None, :]
    d = x - 2 * DIL_SPAN - q
    mult = np.zeros(d.shape, np.int64)
    for window, dil in DIL_PATTERNS:
        reach = (window // 2 // dil) * dil
        mult += ((d % dil == 0) & (np.abs(d) <= reach)).astype(np.int64)
    table = np.where(mult > 0, np.log2(np.maximum(mult, 1)), NEG)
    return jnp.asarray(table, F32)


def _dil_kernel(q_ref, k_ref, v_ref, t_ref, o_ref, *, seq):
    def block(ib):
        t0 = ib * DIL_TQ
        ws = jnp.clip(t0 - DIL_SPAN, 0, seq - DIL_WIN)
        off = pl.multiple_of(ws - t0 + 2 * DIL_SPAN, DIL_TQ)
        ws = pl.multiple_of(ws, DIL_TQ)
        qrows = pl.ds(pl.multiple_of(t0, DIL_TQ), DIL_TQ)
        k = k_ref[0, pl.ds(ws, DIL_WIN), :]
        v = v_ref[0, pl.ds(ws, DIL_WIN), :]
        s = lax.dot_general(q_ref[0, qrows, :], k, (((1,), (1,)), ((), ())),
                            preferred_element_type=F32)
        o_ref[0, qrows, :] = _softmax_pv(s + t_ref[:, pl.ds(off, DIL_WIN)], v)

    def body(i, carry):
        for u in range(ATTN_UNROLL):
            block(i * ATTN_UNROLL + u)
        return carry

    lax.fori_loop(0, seq // DIL_TQ // ATTN_UNROLL, body, 0)


def _dil_attention(qkv, table, d_model):
    b, s, _ = qkv.shape
    heads_total = d_model // HEAD_DIM
    h0 = N_HEADS_NA
    return pl.pallas_call(
        functools.partial(_dil_kernel, seq=s),
        grid=(b, N_HEADS_DIL),
        in_specs=[
            pl.BlockSpec((1, s, HEAD_DIM), lambda bi, h: (bi, 0, h0 + h)),
            pl.BlockSpec((1, s, HEAD_DIM), lambda bi, h: (bi, 0, heads_total + h0 + h)),
            pl.BlockSpec((1, s, HEAD_DIM), lambda bi, h: (bi, 0, 2 * heads_total + h0 + h)),
            pl.BlockSpec(table.shape, lambda bi, h: (0, 0)),
        ],
        out_specs=pl.BlockSpec((1, s, HEAD_DIM), lambda bi, h: (bi, 0, h)),
        out_shape=jax.ShapeDtypeStruct((b, s, N_HEADS_DIL * HEAD_DIM), F32),
        compiler_params=_params(("parallel", "arbitrary"), 48),
        name="dil_attn",
    )(qkv, qkv, qkv, table)


def _oproj_kernel(ona_ref, odil_ref, h_ref, gna_ref, gdil_ref, gt_ref, w_ref,
                  gn_ref, shn_ref, scn_ref, o_ref, xnn_ref):
    d_na = ona_ref.shape[2]
    na = _rms(ona_ref[0], gna_ref[...]).astype(BF16)
    nd = _rms(odil_ref[0], gdil_ref[...]).astype(BF16)
    mix = jnp.dot(na, w_ref[:d_na, :], preferred_element_type=F32)
    mix = mix + jnp.dot(nd, w_ref[d_na:, :], preferred_element_type=F32)
    o_ref[0] = h_ref[0] + gt_ref[0] * mix
    _norm_modulate_rows(o_ref.at[0], gn_ref, shn_ref, scn_ref, xnn_ref.at[0])


def _oproj(o_na, o_dil, h, g_na, g_dil, gate, w_o, next_norm):
    b, s, d = h.shape
    d_na, d_dil = o_na.shape[2], o_dil.shape[2]
    gain, shift, scale = next_norm
    tile = pl.BlockSpec((1, OPROJ_TM, d), lambda bi, i: (bi, i, 0))
    vec = pl.BlockSpec((1, 1, d), lambda bi, i: (bi, 0, 0))
    return pl.pallas_call(
        _oproj_kernel,
        grid=(b, s // OPROJ_TM),
        in_specs=[
            pl.BlockSpec((1, OPROJ_TM, d_na), lambda bi, i: (bi, i, 0)),
            pl.BlockSpec((1, OPROJ_TM, d_dil), lambda bi, i: (bi, i, 0)),
            tile,
            pl.BlockSpec((1, d_na), lambda bi, i: (0, 0)),
            pl.BlockSpec((1, d_dil), lambda bi, i: (0, 0)),
            vec,
            pl.BlockSpec((d, d), lambda bi, i: (0, 0)),
            pl.BlockSpec((1, d), lambda bi, i: (0, 0)), vec, vec,
        ],
        out_specs=[tile, tile],
        out_shape=[jax.ShapeDtypeStruct((b, s, d), F32), jax.ShapeDtypeStruct((b, s, d), BF16)],
        compiler_params=_params(("parallel", "parallel"), 52),
        name="oproj",
    )(o_na, o_dil, h, g_na.reshape(1, d_na), g_dil.reshape(1, d_dil), gate, w_o,
      gain.reshape(1, d), shift, scale)


def kernel(x, c, w_ada, b_ada, g_ffn1, w1_gate, w1_up, w1_down, g_mix, w_qkv, qn_na, kn_na, qn_dil, kn_dil, rpb_na, g_out_na, g_out_dil, w_o, g_ffn2, w2_gate, w2_up, w2_down):
    b, s, d = x.shape
    depth = w_ada.shape[0]
    rows = s // GRID_W
    rope_cos, rope_sin = _rope_tables(s)
    dil_table = _dil_bias_table()
    c_pad = jnp.pad(c, ((0, 8 - b), (0, 0)))
    lane_order = _rotary_lane_order()
    q_factor = ATTN_SCALE * LOG2_E

    h = x
    for l in range(depth):
        mod = _ada(c_pad, w_ada[l], b_ada[l])[:b]
        sh1, sc1, gt1, sh2, sc2, gt2, sh3, sc3, gt3 = [
            m.reshape(b, 1, d) for m in jnp.split(mod, N_MOD, axis=-1)]

        xn = _norm_mod(h, g_ffn1[l], sh1, sc1)
        h, xn = _ffn(xn, h, gt1,
                     w1_gate[l].astype(BF16), w1_up[l].astype(BF16), w1_down[l].astype(BF16),
                     next_norm=(g_mix[l], sh2, sc2))

        head_gains = jnp.stack([
            jnp.stack([qn_na[l] * q_factor, _take_lanes(qn_dil[l], lane_order) * q_factor]),
            jnp.stack([kn_na[l], _take_lanes(kn_dil[l], lane_order)]),
        ]).reshape(2, 2, 1, HEAD_DIM)
        qkv = _qkv(xn, _prep_qkv_weight(w_qkv[l]), head_gains, rope_cos, rope_sin)

        o_na = _na_attention(qkv, _na_bias(rpb_na[l], rows), d)
        o_dil = _dil_attention(qkv, dil_table, d)
        h, xn = _oproj(o_na, o_dil, h, g_out_na[l], g_out_dil[l], gt2, w_o[l].astype(BF16),
                       next_norm=(g_ffn2[l], sh3, sc3))

        h = _ffn(xn, h, gt3,
                 w2_gate[l].astype(BF16), w2_up[l].astype(BF16), w2_down[l].astype(BF16))
    return h
```

```python
import functools
import math

import numpy as np
import jax
import jax.numpy as jnp
from jax import lax
from jax.experimental import pallas as pl
from jax.experimental.pallas import tpu as pltpu

HEAD_DIM = 128
N_HEADS = 16
N_HEADS_NA = 8
N_HEADS_DIL = N_HEADS - N_HEADS_NA
GRID_W = 64
NA_ROWS = 8
NA_COLS = 16
DIL_PATTERNS = ((128, 1), (512, 4), (2048, 16))
ROPE_THETA = 500000.0
ROPE_DIM = HEAD_DIM // 4
N_MOD = 9
EPS = 1e-6
NEG = -1e30
ATTN_SCALE = HEAD_DIM ** -0.5
LOG2_E = math.log2(math.e)

BF16 = jnp.bfloat16
F32 = jnp.float32

MIB = 1024 * 1024

ADA_TN = 1024
NORM_TM = 512
NORM_ROWS = 32
NORM_UNROLL = 4
FFN_TM = 1024
FFN_TF = 512
FFN_RES = 256
QKV_TM = 512
QKV_TN = 2048
QKV_PIECE = 256
SIDE_CAST_COLS = 512
PREP_ROWS = 512
OPROJ_TM = 512
ATTN_UNROLL = 4
NA_RB = 4
NA_KR = 12
DIL_TQ = 256
DIL_SPAN = 1024
DIL_WIN = DIL_TQ + 2 * DIL_SPAN


def _params(sem, vmem_mib):
    return pltpu.CompilerParams(dimension_semantics=sem, vmem_limit_bytes=vmem_mib * MIB)


def _silu(x):
    return x * (1.0 / (1.0 + jnp.exp(-x)))


def _rms(x, gain):
    ms = jnp.mean(x * x, axis=-1, keepdims=True)
    return x * lax.rsqrt(ms + EPS) * gain


def _norm_modulate_rows(x_ref, g_ref, sh_ref, sc_ref, xn_ref):
    n_rows = xn_ref.shape[0]
    gain = g_ref[...]
    scale1 = 1.0 + sc_ref[0]
    shift = sh_ref[0]

    def body(i, carry):
        rows = pl.ds(pl.multiple_of(i * NORM_ROWS, NORM_ROWS), NORM_ROWS)
        xn_ref[rows, :] = (_rms(x_ref[rows, :], gain) * scale1 + shift).astype(BF16)
        return carry

    lax.fori_loop(0, n_rows // NORM_ROWS, body, 0, unroll=NORM_UNROLL)


def _norm_mod_kernel(x_ref, g_ref, sh_ref, sc_ref, xn_ref):
    _norm_modulate_rows(x_ref.at[0], g_ref, sh_ref, sc_ref, xn_ref.at[0])


def _norm_mod(x, gain, shift, scale):
    b, s, d = x.shape
    tile = pl.BlockSpec((1, NORM_TM, d), lambda bi, i: (bi, i, 0))
    vec = pl.BlockSpec((1, 1, d), lambda bi, i: (bi, 0, 0))
    return pl.pallas_call(
        _norm_mod_kernel,
        grid=(b, s // NORM_TM),
        in_specs=[tile, pl.BlockSpec((1, d), lambda bi, i: (0, 0)), vec, vec],
        out_specs=tile,
        out_shape=jax.ShapeDtypeStruct((b, s, d), BF16),
        compiler_params=_params(("parallel", "parallel"), 24),
        name="norm_mod",
    )(x, gain.reshape(1, d), shift, scale)


def _ada_kernel(c_ref, w_ref, b_ref, o_ref):
    a = _silu(c_ref[...]).astype(BF16)
    w = w_ref[...].astype(BF16)
    o_ref[...] = jnp.dot(a, w, preferred_element_type=F32) + b_ref[...]


def _ada(c_pad, w_ada, b_ada):
    rows, d = c_pad.shape
    n = w_ada.shape[1]
    return pl.pallas_call(
        _ada_kernel,
        grid=(n // ADA_TN,),
        in_specs=[
            pl.BlockSpec((rows, d), lambda j: (0, 0)),
            pl.BlockSpec((d, ADA_TN), lambda j: (0, j)),
            pl.BlockSpec((1, ADA_TN), lambda j: (0, j)),
        ],
        out_specs=pl.BlockSpec((rows, ADA_TN), lambda j: (0, j)),
        out_shape=jax.ShapeDtypeStruct((rows, n), F32),
        compiler_params=_params(("arbitrary",), 24),
        name="ada_mod",
    )(c_pad, w_ada, b_ada.reshape(1, n))


def _ffn_kernel(xn_ref, xres_ref, gt_ref, wg_ref, wu_ref, wd_ref, *rest, emit_next):
    if emit_next:
        gn_ref, shn_ref, scn_ref, o_ref, xnn_ref = rest
    else:
        (o_ref,) = rest
    j = pl.program_id(2)
    n_res = o_ref.shape[2] // FFN_RES

    @pl.when(j == 0)
    def _():
        o_ref[0] = jnp.zeros(o_ref.shape[1:], F32)

    xn = xn_ref[0]
    g = jnp.dot(xn, wg_ref[...], preferred_element_type=F32)
    u = jnp.dot(xn, wu_ref[...], preferred_element_type=F32)
    a = (_silu(g) * u).astype(BF16)
    o_ref[0] += (0.5 * gt_ref[0]) * jnp.dot(a, wd_ref[...], preferred_element_type=F32)

    @pl.when(j < n_res)
    def _():
        cols = pl.ds(pl.multiple_of(j * FFN_RES, FFN_RES), FFN_RES)
        o_ref[0, :, cols] += xres_ref[0]

    if emit_next:
        @pl.when(j == pl.num_programs(2) - 1)
        def _():
            _norm_modulate_rows(o_ref.at[0], gn_ref, shn_ref, scn_ref, xnn_ref.at[0])


def _ffn(xn, x, gate, w_gate, w_up, w_down, next_norm=None):
    b, s, d = x.shape
    dff = w_gate.shape[1]
    n_steps = dff // FFN_TF
    n_res = d // FFN_RES
    assert n_res <= n_steps
    tile = pl.BlockSpec((1, FFN_TM, d), lambda bi, i, j: (bi, i, 0))
    vec = pl.BlockSpec((1, 1, d), lambda bi, i, j: (bi, 0, 0))
    in_specs = [
        tile,
        pl.BlockSpec((1, FFN_TM, FFN_RES), lambda bi, i, j: (bi, i, jnp.minimum(j, n_res - 1))),
        vec,
        pl.BlockSpec((d, FFN_TF), lambda bi, i, j: (0, j)),
        pl.BlockSpec((d, FFN_TF), lambda bi, i, j: (0, j)),
        pl.BlockSpec((FFN_TF, d), lambda bi, i, j: (j, 0)),
    ]
    args = [xn, x, gate, w_gate, w_up, w_down]
    out_specs = [tile]
    out_shape = [jax.ShapeDtypeStruct((b, s, d), F32)]
    if next_norm is not None:
        gain, shift, scale = next_norm
        in_specs += [pl.BlockSpec((1, d), lambda bi, i, j: (0, 0)), vec, vec]
        args += [gain.reshape(1, d), shift, scale]
        out_specs.append(tile)
        out_shape.append(jax.ShapeDtypeStruct((b, s, d), BF16))
    outs = pl.pallas_call(
        functools.partial(_ffn_kernel, emit_next=next_norm is not None),
        grid=(b, s // FFN_TM, n_steps),
        in_specs=in_specs,
        out_specs=out_specs,
        out_shape=out_shape,
        compiler_params=_params(("parallel", "parallel", "arbitrary"), 56),
        name="ffn",
    )(*args)
    return outs if next_norm is not None else outs[0]


def _qkv_kernel(xn_ref, w_ref, hg_ref, cos_ref, sin_ref, o_ref):
    j = pl.program_id(2)
    n_pieces = QKV_TN // QKV_PIECE
    heads_per_piece = QKV_PIECE // HEAD_DIM

    def normed(r, head):
        return _rms(r, hg_ref[0, head // N_HEADS_NA])

    def normed_rotary(r, head):
        y = normed(r, head)
        return y * cos_ref[...] + pltpu.roll(y, HEAD_DIM // 2, 1) * sin_ref[...]

    def pieces(epilogue_of_head):
        xn = xn_ref[0]
        for p in list(range(n_pieces // 2, n_pieces)) + list(range(n_pieces // 2)):
            r = jnp.dot(xn, w_ref[:, p * QKV_PIECE:(p + 1) * QKV_PIECE],
                        preferred_element_type=F32)
            for hh in range(heads_per_piece):
                head = p * heads_per_piece + hh
                dst = slice(head * HEAD_DIM, (head + 1) * HEAD_DIM)
                y = epilogue_of_head(head)(r[:, hh * HEAD_DIM:(hh + 1) * HEAD_DIM], head)
                o_ref[0, :, dst] = y.astype(BF16)

    @pl.when(j < 2)
    def _():
        pieces(lambda head: normed if head < N_HEADS_NA else normed_rotary)

    @pl.when(j == 2)
    def _():
        pieces(lambda head: (lambda r, _: r))


def _qkv(xn, w_qkv, head_gains, rope_cos, rope_sin):
    b, s, d = xn.shape
    n = w_qkv.shape[1]
    assert QKV_TN == N_HEADS * HEAD_DIM and n == 3 * QKV_TN
    rope = pl.BlockSpec((QKV_TM, HEAD_DIM), lambda bi, i, j: (i, 0))
    return pl.pallas_call(
        _qkv_kernel,
        grid=(b, s // QKV_TM, n // QKV_TN),
        in_specs=[
            pl.BlockSpec((1, QKV_TM, d), lambda bi, i, j: (bi, i, 0)),
            pl.BlockSpec((d, QKV_TN), lambda bi, i, j: (0, j)),
            pl.BlockSpec((1, 2, 1, HEAD_DIM), lambda bi, i, j: (jnp.minimum(j, 1), 0, 0, 0)),
            rope, rope,
        ],
        out_specs=pl.BlockSpec((1, QKV_TM, QKV_TN), lambda bi, i, j: (bi, i, j)),
        out_shape=jax.ShapeDtypeStruct((b, s, n), BF16),
        compiler_params=_params(("parallel", "parallel", "arbitrary"), 48),
        name="qkv",
    )(xn, w_qkv, head_gains, rope_cos, rope_sin)


def _rotary_lane_order():
    half = ROPE_DIM // 2
    mid = HEAD_DIM // 2
    order = (list(range(half)) + list(range(ROPE_DIM, ROPE_DIM + mid - half))
             + list(range(half, ROPE_DIM)) + list(range(ROPE_DIM + mid - half, HEAD_DIM)))
    assert sorted(order) == list(range(HEAD_DIM))
    return order


def _take_lanes(a, order):
    runs, start = [], 0
    for i in range(1, len(order) + 1):
        if i == len(order) or order[i] != order[i - 1] + 1:
            runs.append(a[..., order[start]:order[i - 1] + 1])
            start = i
    return jnp.concatenate(runs, axis=-1)


def _rope_tables(s):
    pos = jnp.arange(s, dtype=F32)
    inv = jnp.power(ROPE_THETA, -jnp.arange(0, ROPE_DIM, 2, dtype=F32) / ROPE_DIM)
    ang = pos[:, None] * inv[None, :]
    cos, sin = jnp.cos(ang), jnp.sin(ang)
    half = ROPE_DIM // 2
    gap = HEAD_DIM // 2 - half
    ones, zeros = jnp.ones((s, gap), F32), jnp.zeros((s, gap), F32)
    rope_cos = jnp.concatenate([cos, ones, cos, ones], axis=1)
    rope_sin = jnp.concatenate([-sin, zeros, sin, zeros], axis=1)
    return rope_cos, rope_sin


def _lane_runs(order):
    runs, start = [], 0
    for i in range(1, len(order) + 1):
        if i == len(order) or order[i] != order[i - 1] + 1:
            runs.append((start, order[start], i - start))
            start = i
    return runs


def _prep_qkv_kernel(w_ref, o_ref):
    j = pl.program_id(1)

    @pl.when(j == 2)
    def _():
        o_ref[...] = w_ref[...].astype(BF16)

    @pl.when(j < 2)
    def _():
        lane = lax.broadcasted_iota(jnp.int32, (w_ref.shape[0], HEAD_DIM), 1)
        runs = _lane_runs(_rotary_lane_order())
        for head in range(N_HEADS):
            cols = slice(head * HEAD_DIM, (head + 1) * HEAD_DIM)
            x = w_ref[:, cols]
            if head >= N_HEADS_NA:
                y = x
                for dst, src, length in runs:
                    if dst != src:
                        moved = pltpu.roll(x, (dst - src) % HEAD_DIM, 1)
                        y = jnp.where(jnp.logical_and(lane >= dst, lane < dst + length), moved, y)
                x = y
            o_ref[:, cols] = x.astype(BF16)


def _prep_qkv_weight(w_qkv):
    d, n = w_qkv.shape
    assert n == 3 * QKV_TN
    block = pl.BlockSpec((PREP_ROWS, QKV_TN), lambda i, j: (i, j))
    return pl.pallas_call(
        _prep_qkv_kernel,
        grid=(d // PREP_ROWS, 3),
        in_specs=[block],
        out_specs=block,
        out_shape=jax.ShapeDtypeStruct((d, n), BF16),
        compiler_params=_params(("parallel", "arbitrary"), 24),
        name="prep_qkv_weight",
    )(w_qkv)


def _na_window_start(ib, rows):
    return jnp.clip(NA_RB * ib - NA_ROWS // 2, 0, rows - NA_KR)


def _na_bias_kernel(rpb_ref, o_ref, *, rows):
    h = pl.program_id(0)
    n_ro = 2 * NA_ROWS - 1
    n_co = 2 * NA_COLS - 1
    c = lax.broadcasted_iota(jnp.int32, (GRID_W, 2 * GRID_W), 0)
    lane = lax.broadcasted_iota(jnp.int32, (GRID_W, 2 * GRID_W), 1)
    kc = lane & (GRID_W - 1)
    diff = kc - c + (NA_COLS - 1)
    cs = jnp.clip(c - NA_COLS // 2, 0, GRID_W - NA_COLS)
    col_ok = jnp.logical_and(kc >= cs, kc < cs + NA_COLS)
    neg = jnp.full((GRID_W, 2 * GRID_W), NEG, F32)
    toeplitz = []
    for ro in range(n_ro):
        t = jnp.zeros((GRID_W, 2 * GRID_W), F32)
        for m in range(n_co):
            t = jnp.where(diff == m, rpb_ref[(h * n_ro + ro) * n_co + m], t)
        toeplitz.append(jnp.where(col_ok, t * LOG2_E, neg))
    low_half = lane < GRID_W
    n_blocks = rows // NA_RB
    for var, ib in enumerate((0, n_blocks // 2, n_blocks - 1)):
        ws = min(max(NA_RB * ib - NA_ROWS // 2, 0), rows - NA_KR)
        for qr in range(NA_RB):
            r = NA_RB * ib + qr
            r_start = min(max(r - NA_ROWS // 2, 0), rows - NA_ROWS)
            for g in range(NA_KR // 2):
                pieces = []
                for kr in (2 * g, 2 * g + 1):
                    ka = ws + kr
                    inside = r_start <= ka < r_start + NA_ROWS
                    pieces.append(toeplitz[ka - r + NA_ROWS - 1] if inside else neg)
                o_ref[0, var, qr * GRID_W:(qr + 1) * GRID_W, g * 2 * GRID_W:(g + 1) * 2 * GRID_W] = (
                    jnp.where(low_half, pieces[0], pieces[1]))


def _na_bias(rpb, rows):
    nh = rpb.shape[0]
    return pl.pallas_call(
        functools.partial(_na_bias_kernel, rows=rows),
        grid=(nh,),
        in_specs=[pl.BlockSpec(memory_space=pltpu.SMEM)],
        out_specs=pl.BlockSpec((1, 3, NA_RB * GRID_W, NA_KR * GRID_W), lambda h: (h, 0, 0, 0)),
        out_shape=jax.ShapeDtypeStruct((nh, 3, NA_RB * GRID_W, NA_KR * GRID_W), F32),
        compiler_params=_params(("arbitrary",), 24),
        name="na_bias",
    )(rpb.reshape(-1))


def _softmax_pv(s2, v):
    m = jnp.max(s2, axis=-1, keepdims=True)
    p = jnp.exp2(s2 - m)
    l = jnp.sum(p, axis=-1, keepdims=True)
    return jnp.dot(p.astype(BF16), v, preferred_element_type=F32) / l


def _side_cast(refs):
    n = len(refs) // 2
    for src_ref, dst_ref in zip(refs[:n], refs[n:]):
        for c0 in range(0, src_ref.shape[1], SIDE_CAST_COLS):
            cols = slice(c0, c0 + SIDE_CAST_COLS)
            dst_ref[:, cols] = src_ref[:, cols].astype(BF16)


def _side_cast_plumbing(weights, n_steps, step_of):
    specs, shapes = [], []
    for w in weights:
        slab = w.shape[0] // n_steps
        assert slab * n_steps == w.shape[0] and w.shape[1] % SIDE_CAST_COLS == 0
        specs.append(pl.BlockSpec((slab, w.shape[1]), lambda *g: (step_of(*g), 0)))
        shapes.append(jax.ShapeDtypeStruct(w.shape, BF16))
    return specs, shapes


def _na_kernel(q_ref, k_ref, v_ref, bias_ref, *rest, rows):
    n_side = (len(rest) - 1) // 2
    o_ref = rest[n_side]
    _side_cast(rest[:n_side] + rest[n_side + 1:])
    n_blocks = rows // NA_RB
    tq = NA_RB * GRID_W

    def block(ib):
        start = pl.multiple_of(_na_window_start(ib, rows) * GRID_W, GRID_W)
        variant = jnp.where(ib == 0, 0, jnp.where(ib == n_blocks - 1, 2, 1))
        qrows = pl.ds(pl.multiple_of(ib * tq, tq), tq)
        k = k_ref[0, pl.ds(start, NA_KR * GRID_W), :]
        v = v_ref[0, pl.ds(start, NA_KR * GRID_W), :]
        s = lax.dot_general(q_ref[0, qrows, :], k, (((1,), (1,)), ((), ())),
                            preferred_element_type=F32)
        o_ref[0, qrows, :] = _softmax_pv(s + bias_ref[0, variant], v)

    def body(i, carry):
        for u in range(ATTN_UNROLL):
            block(i * ATTN_UNROLL + u)
        return carry

    lax.fori_loop(0, n_blocks // ATTN_UNROLL, body, 0)


def _na_attention(qkv, bias, d_model, side_weights=()):
    b, s, _ = qkv.shape
    rows = s // GRID_W
    heads_total = d_model // HEAD_DIM
    tq = NA_RB * GRID_W
    side_specs, side_shapes = _side_cast_plumbing(
        side_weights, b * N_HEADS_NA, lambda bi, h: bi * N_HEADS_NA + h)
    return pl.pallas_call(
        functools.partial(_na_kernel, rows=rows),
        grid=(b, N_HEADS_NA),
        in_specs=[
            pl.BlockSpec((1, s, HEAD_DIM), lambda bi, h: (bi, 0, h)),
            pl.BlockSpec((1, s, HEAD_DIM), lambda bi, h: (bi, 0, heads_total + h)),
            pl.BlockSpec((1, s, HEAD_DIM), lambda bi, h: (bi, 0, 2 * heads_total + h)),
            pl.BlockSpec((1, 3, tq, NA_KR * GRID_W), lambda bi, h: (h, 0, 0, 0)),
        ] + side_specs,
        out_specs=[pl.BlockSpec((1, s, HEAD_DIM), lambda bi, h: (bi, 0, h))] + side_specs,
        out_shape=[jax.ShapeDtypeStruct((b, s, N_HEADS_NA * HEAD_DIM), F32)] + side_shapes,
        compiler_params=_params(("parallel", "arbitrary"), 52),
        name="na_attn",
    )(qkv, qkv, qkv, bias, *side_weights)


def _dil_bias_table():
    q = np.arange(DIL_TQ)[:, None]
    x = np.arange(DIL_WIN + 2 * DIL_SPAN)[None, :]
    d = x - 2 * DIL_SPAN - q
    mult = np.zeros(d.shape, np.int64)
    for window, dil in DIL_PATTERNS:
        reach = (window // 2 // dil) * dil
        mult += ((d % dil == 0) & (np.abs(d) <= reach)).astype(np.int64)
    table = np.where(mult > 0, np.log2(np.maximum(mult, 1)), NEG)
    return jnp.asarray(table, F32)


def _dil_kernel(q_ref, k_ref, v_ref, t_ref, *rest, seq):
    n_side = (len(rest) - 1) // 2
    o_ref = rest[n_side]
    _side_cast(rest[:n_side] + rest[n_side + 1:])

    def block(ib):
        t0 = ib * DIL_TQ
        ws = jnp.clip(t0 - DIL_SPAN, 0, seq - DIL_WIN)
        off = pl.multiple_of(ws - t0 + 2 * DIL_SPAN, DIL_TQ)
        ws = pl.multiple_of(ws, DIL_TQ)
        qrows = pl.ds(pl.multiple_of(t0, DIL_TQ), DIL_TQ)
        k = k_ref[0, pl.ds(ws, DIL_WIN), :]
        v = v_ref[0, pl.ds(ws, DIL_WIN), :]
        s = lax.dot_general(q_ref[0, qrows, :], k, (((1,), (1,)), ((), ())),
                            preferred_element_type=F32)
        o_ref[0, qrows, :] = _softmax_pv(s + t_ref[:, pl.ds(off, DIL_WIN)], v)

    def body(i, carry):
        for u in range(ATTN_UNROLL):
            block(i * ATTN_UNROLL + u)
        return carry

    lax.fori_loop(0, seq // DIL_TQ // ATTN_UNROLL, body, 0)


def _dil_attention(qkv, table, d_model, side_weights=()):
    b, s, _ = qkv.shape
    heads_total = d_model // HEAD_DIM
    h0 = N_HEADS_NA
    side_specs, side_shapes = _side_cast_plumbing(
        side_weights, b * N_HEADS_DIL, lambda bi, h: bi * N_HEADS_DIL + h)
    return pl.pallas_call(
        functools.partial(_dil_kernel, seq=s),
        grid=(b, N_HEADS_DIL),
        in_specs=[
            pl.BlockSpec((1, s, HEAD_DIM), lambda bi, h: (bi, 0, h0 + h)),
            pl.BlockSpec((1, s, HEAD_DIM), lambda bi, h: (bi, 0, heads_total + h0 + h)),
            pl.BlockSpec((1, s, HEAD_DIM), lambda bi, h: (bi, 0, 2 * heads_total + h0 + h)),
            pl.BlockSpec(table.shape, lambda bi, h: (0, 0)),
        ] + side_specs,
        out_specs=[pl.BlockSpec((1, s, HEAD_DIM), lambda bi, h: (bi, 0, h))] + side_specs,
        out_shape=[jax.ShapeDtypeStruct((b, s, N_HEADS_DIL * HEAD_DIM), F32)] + side_shapes,
        compiler_params=_params(("parallel", "arbitrary"), 52),
        name="dil_attn",
    )(qkv, qkv, qkv, table, *side_weights)


def _oproj_kernel(ona_ref, odil_ref, h_ref, gna_ref, gdil_ref, gt_ref, w_ref,
                  gn_ref, shn_ref, scn_ref, o_ref, xnn_ref):
    d_na = ona_ref.shape[2]
    na = _rms(ona_ref[0], gna_ref[...]).astype(BF16)
    nd = _rms(odil_ref[0], gdil_ref[...]).astype(BF16)
    mix = jnp.dot(na, w_ref[:d_na, :], preferred_element_type=F32)
    mix = mix + jnp.dot(nd, w_ref[d_na:, :], preferred_element_type=F32)
    o_ref[0] = h_ref[0] + gt_ref[0] * mix
    _norm_modulate_rows(o_ref.at[0], gn_ref, shn_ref, scn_ref, xnn_ref.at[0])


def _oproj(o_na, o_dil, h, g_na, g_dil, gate, w_o, next_norm):
    b, s, d = h.shape
    d_na, d_dil = o_na.shape[2], o_dil.shape[2]
    gain, shift, scale = next_norm
    tile = pl.BlockSpec((1, OPROJ_TM, d), lambda bi, i: (bi, i, 0))
    vec = pl.BlockSpec((1, 1, d), lambda bi, i: (bi, 0, 0))
    return pl.pallas_call(
        _oproj_kernel,
        grid=(b, s // OPROJ_TM),
        in_specs=[
            pl.BlockSpec((1, OPROJ_TM, d_na), lambda bi, i: (bi, i, 0)),
            pl.BlockSpec((1, OPROJ_TM, d_dil), lambda bi, i: (bi, i, 0)),
            tile,
            pl.BlockSpec((1, d_na), lambda bi, i: (0, 0)),
            pl.BlockSpec((1, d_dil), lambda bi, i: (0, 0)),
            vec,
            pl.BlockSpec((d, d), lambda bi, i: (0, 0)),
            pl.BlockSpec((1, d), lambda bi, i: (0, 0)), vec, vec,
        ],
        out_specs=[tile, tile],
        out_shape=[jax.ShapeDtypeStruct((b, s, d), F32), jax.ShapeDtypeStruct((b, s, d), BF16)],
        compiler_params=_params(("parallel", "parallel"), 52),
        name="oproj",
    )(o_na, o_dil, h, g_na.reshape(1, d_na), g_dil.reshape(1, d_dil), gate, w_o,
      gain.reshape(1, d), shift, scale)


def kernel(x, c, w_ada, b_ada, g_ffn1, w1_gate, w1_up, w1_down, g_mix, w_qkv, qn_na, kn_na, qn_dil, kn_dil, rpb_na, g_out_na, g_out_dil, w_o, g_ffn2, w2_gate, w2_up, w2_down):
    b, s, d = x.shape
    depth = w_ada.shape[0]
    rows = s // GRID_W
    rope_cos, rope_sin = _rope_tables(s)
    dil_table = _dil_bias_table()
    c_pad = jnp.pad(c, ((0, 8 - b), (0, 0)))
    lane_order = _rotary_lane_order()
    q_factor = ATTN_SCALE * LOG2_E

    h = x
    for l in range(depth):
        mod = _ada(c_pad, w_ada[l], b_ada[l])[:b]
        sh1, sc1, gt1, sh2, sc2, gt2, sh3, sc3, gt3 = [
            m.reshape(b, 1, d) for m in jnp.split(mod, N_MOD, axis=-1)]

        xn = _norm_mod(h, g_ffn1[l], sh1, sc1)
        h, xn = _ffn(xn, h, gt1,
                     w1_gate[l].astype(BF16), w1_up[l].astype(BF16), w1_down[l].astype(BF16),
                     next_norm=(g_mix[l], sh2, sc2))

        head_gains = jnp.stack([
            jnp.stack([qn_na[l] * q_factor, _take_lanes(qn_dil[l], lane_order) * q_factor]),
            jnp.stack([kn_na[l], _take_lanes(kn_dil[l], lane_order)]),
        ]).reshape(2, 2, 1, HEAD_DIM)
        qkv = _qkv(xn, _prep_qkv_weight(w_qkv[l]), head_gains, rope_cos, rope_sin)

        o_na, w2g, w2u, wo = _na_attention(qkv, _na_bias(rpb_na[l], rows), d,
                                           side_weights=(w2_gate[l], w2_up[l], w_o[l]))
        o_dil, w2d = _dil_attention(qkv, dil_table, d, side_weights=(w2_down[l],))
        h, xn = _oproj(o_na, o_dil, h, g_out_na[l], g_out_dil[l], gt2, wo,
                       next_norm=(g_ffn2[l], sh3, sc3))

        h = _ffn(xn, h, gt3, w2g, w2u, w2d)
    return h
```

```python
import functools
import math

import numpy as np
import jax
import jax.numpy as jnp
from jax import lax
from jax.experimental import pallas as pl
from jax.experimental.pallas import tpu as pltpu

HEAD_DIM = 128
N_HEADS = 16
N_HEADS_NA = 8
N_HEADS_DIL = N_HEADS - N_HEADS_NA
GRID_W = 64
NA_ROWS = 8
NA_COLS = 16
DIL_PATTERNS = ((128, 1), (512, 4), (2048, 16))
ROPE_THETA = 500000.0
ROPE_DIM = HEAD_DIM // 4
N_MOD = 9
EPS = 1e-6
NEG = -1e30
ATTN_SCALE = HEAD_DIM ** -0.5
LOG2_E = math.log2(math.e)

BF16 = jnp.bfloat16
F32 = jnp.float32

MIB = 1024 * 1024

ADA_TN = 1024
NORM_TM = 512
NORM_ROWS = 32
NORM_UNROLL = 4
FFN_TM = 1024
FFN_TF = 512
FFN_RES = 256
QKV_TM = 512
QKV_TN = 2048
QKV_PIECE = 256
SIDE_CAST_COLS = 512
PREP_ROWS = 512
OPROJ_TM = 512
ATTN_UNROLL = 4
NA_RB = 4
NA_KR = 12
DIL_TQ = 256
DIL_SPAN = 1024
DIL_WIN = DIL_TQ + 2 * DIL_SPAN


def _params(sem, vmem_mib):
    return pltpu.CompilerParams(dimension_semantics=sem, vmem_limit_bytes=vmem_mib * MIB)


def _silu(x):
    return x * (1.0 / (1.0 + jnp.exp(-x)))


def _rms(x, gain):
    ms = jnp.mean(x * x, axis=-1, keepdims=True)
    return x * lax.rsqrt(ms + EPS) * gain


def _norm_modulate_rows(x_ref, g_ref, sh_ref, sc_ref, xn_ref):
    n_rows = xn_ref.shape[0]
    gain_scale = g_ref[...] * (1.0 + sc_ref[0])
    shift = sh_ref[0]

    def body(i, carry):
        rows = pl.ds(pl.multiple_of(i * NORM_ROWS, NORM_ROWS), NORM_ROWS)
        xn_ref[rows, :] = (_rms(x_ref[rows, :], gain_scale) + shift).astype(BF16)
        return carry

    lax.fori_loop(0, n_rows // NORM_ROWS, body, 0, unroll=NORM_UNROLL)


def _norm_mod_kernel(x_ref, g_ref, sh_ref, sc_ref, xn_ref):
    _norm_modulate_rows(x_ref.at[0], g_ref, sh_ref, sc_ref, xn_ref.at[0])


def _norm_mod(x, gain, shift, scale):
    b, s, d = x.shape
    tile = pl.BlockSpec((1, NORM_TM, d), lambda bi, i: (bi, i, 0))
    vec = pl.BlockSpec((1, 1, d), lambda bi, i: (bi, 0, 0))
    return pl.pallas_call(
        _norm_mod_kernel,
        grid=(b, s // NORM_TM),
        in_specs=[tile, pl.BlockSpec((1, d), lambda bi, i: (0, 0)), vec, vec],
        out_specs=tile,
        out_shape=jax.ShapeDtypeStruct((b, s, d), BF16),
        compiler_params=_params(("parallel", "parallel"), 24),
        name="norm_mod",
    )(x, gain.reshape(1, d), shift, scale)


def _ada_kernel(c_ref, w_ref, b_ref, o_ref):
    a = _silu(c_ref[...]).astype(BF16)
    w = w_ref[...].astype(BF16)
    o_ref[...] = jnp.dot(a, w, preferred_element_type=F32) + b_ref[...]


def _ada(c_pad, w_ada, b_ada):
    rows, d = c_pad.shape
    n = w_ada.shape[1]
    return pl.pallas_call(
        _ada_kernel,
        grid=(n // ADA_TN,),
        in_specs=[
            pl.BlockSpec((rows, d), lambda j: (0, 0)),
            pl.BlockSpec((d, ADA_TN), lambda j: (0, j)),
            pl.BlockSpec((1, ADA_TN), lambda j: (0, j)),
        ],
        out_specs=pl.BlockSpec((rows, ADA_TN), lambda j: (0, j)),
        out_shape=jax.ShapeDtypeStruct((rows, n), F32),
        compiler_params=_params(("arbitrary",), 24),
        name="ada_mod",
    )(c_pad, w_ada, b_ada.reshape(1, n))


def _ffn_kernel(xn_ref, xres_ref, gt_ref, wg_ref, wu_ref, wd_ref, *rest, emit_next):
    if emit_next:
        gn_ref, shn_ref, scn_ref, o_ref, xnn_ref = rest
    else:
        (o_ref,) = rest
    j = pl.program_id(2)
    n_res = o_ref.shape[2] // FFN_RES

    @pl.when(j == 0)
    def _():
        o_ref[0] = jnp.zeros(o_ref.shape[1:], F32)

    xn = xn_ref[0]
    g = jnp.dot(xn, wg_ref[...], preferred_element_type=F32)
    u = jnp.dot(xn, wu_ref[...], preferred_element_type=F32)
    a = (_silu(g) * u).astype(BF16)
    o_ref[0] += (0.5 * gt_ref[0]) * jnp.dot(a, wd_ref[...], preferred_element_type=F32)

    @pl.when(j < n_res)
    def _():
        cols = pl.ds(pl.multiple_of(j * FFN_RES, FFN_RES), FFN_RES)
        o_ref[0, :, cols] += xres_ref[0]

    if emit_next:
        @pl.when(j == pl.num_programs(2) - 1)
        def _():
            _norm_modulate_rows(o_ref.at[0], gn_ref, shn_ref, scn_ref, xnn_ref.at[0])


def _ffn(xn, x, gate, w_gate, w_up, w_down, next_norm=None):
    b, s, d = x.shape
    dff = w_gate.shape[1]
    n_steps = dff // FFN_TF
    n_res = d // FFN_RES
    assert n_res <= n_steps
    tile = pl.BlockSpec((1, FFN_TM, d), lambda bi, i, j: (bi, i, 0))
    vec = pl.BlockSpec((1, 1, d), lambda bi, i, j: (bi, 0, 0))
    in_specs = [
        tile,
        pl.BlockSpec((1, FFN_TM, FFN_RES), lambda bi, i, j: (bi, i, jnp.minimum(j, n_res - 1))),
        vec,
        pl.BlockSpec((d, FFN_TF), lambda bi, i, j: (0, j)),
        pl.BlockSpec((d, FFN_TF), lambda bi, i, j: (0, j)),
        pl.BlockSpec((FFN_TF, d), lambda bi, i, j: (j, 0)),
    ]
    args = [xn, x, gate, w_gate, w_up, w_down]
    out_specs = [tile]
    out_shape = [jax.ShapeDtypeStruct((b, s, d), F32)]
    if next_norm is not None:
        gain, shift, scale = next_norm
        in_specs += [pl.BlockSpec((1, d), lambda bi, i, j: (0, 0)), vec, vec]
        args += [gain.reshape(1, d), shift, scale]
        out_specs.append(tile)
        out_shape.append(jax.ShapeDtypeStruct((b, s, d), BF16))
    outs = pl.pallas_call(
        functools.partial(_ffn_kernel, emit_next=next_norm is not None),
        grid=(b, s // FFN_TM, n_steps),
        in_specs=in_specs,
        out_specs=out_specs,
        out_shape=out_shape,
        compiler_params=_params(("parallel", "parallel", "arbitrary"), 56),
        name="ffn",
    )(*args)
    return outs if next_norm is not None else outs[0]


def _qkv_kernel(xn_ref, w_ref, hg_ref, cos_ref, sin_ref, o_ref):
    j = pl.program_id(2)
    n_pieces = QKV_TN // QKV_PIECE
    heads_per_piece = QKV_PIECE // HEAD_DIM

    def normed(r, head):
        return _rms(r, hg_ref[0, head // N_HEADS_NA])

    def normed_rotary(r, head):
        y = normed(r, head)
        return y * cos_ref[...] + pltpu.roll(y, HEAD_DIM // 2, 1) * sin_ref[...]

    def pieces(epilogue_of_head):
        xn = xn_ref[0]
        for p in list(range(n_pieces // 2, n_pieces)) + list(range(n_pieces // 2)):
            r = jnp.dot(xn, w_ref[:, p * QKV_PIECE:(p + 1) * QKV_PIECE],
                        preferred_element_type=F32)
            for hh in range(heads_per_piece):
                head = p * heads_per_piece + hh
                y = epilogue_of_head(head)(r[:, hh * HEAD_DIM:(hh + 1) * HEAD_DIM], head)
                o_ref[head] = y.astype(BF16)

    @pl.when(j < 2)
    def _():
        pieces(lambda head: normed if head < N_HEADS_NA else normed_rotary)

    @pl.when(j == 2)
    def _():
        pieces(lambda head: (lambda r, _: r))


def _qkv(xn, w_qkv, head_gains, rope_cos, rope_sin):
    b, s, d = xn.shape
    n = w_qkv.shape[1]
    assert QKV_TN == N_HEADS * HEAD_DIM and n == 3 * QKV_TN
    rope = pl.BlockSpec((QKV_TM, HEAD_DIM), lambda bi, i, j: (i, 0))
    return pl.pallas_call(
        _qkv_kernel,
        grid=(b, s // QKV_TM, n // QKV_TN),
        in_specs=[
            pl.BlockSpec((1, QKV_TM, d), lambda bi, i, j: (bi, i, 0)),
            pl.BlockSpec((d, QKV_TN), lambda bi, i, j: (0, j)),
            pl.BlockSpec((1, 2, 1, HEAD_DIM), lambda bi, i, j: (jnp.minimum(j, 1), 0, 0, 0)),
            rope, rope,
        ],
        out_specs=pl.BlockSpec((None, N_HEADS, QKV_TM, HEAD_DIM), lambda bi, i, j: (bi, j, i, 0)),
        out_shape=jax.ShapeDtypeStruct((b, 3 * N_HEADS, s, HEAD_DIM), BF16),
        compiler_params=_params(("parallel", "parallel", "arbitrary"), 48),
        name="qkv",
    )(xn, w_qkv, head_gains, rope_cos, rope_sin)


def _rotary_lane_order():
    half = ROPE_DIM // 2
    mid = HEAD_DIM // 2
    order = (list(range(half)) + list(range(ROPE_DIM, ROPE_DIM + mid - half))
             + list(range(half, ROPE_DIM)) + list(range(ROPE_DIM + mid - half, HEAD_DIM)))
    assert sorted(order) == list(range(HEAD_DIM))
    return order


def _take_lanes(a, order):
    runs, start = [], 0
    for i in range(1, len(order) + 1):
        if i == len(order) or order[i] != order[i - 1] + 1:
            runs.append(a[..., order[start]:order[i - 1] + 1])
            start = i
    return jnp.concatenate(runs, axis=-1)


def _rope_tables(s):
    pos = jnp.arange(s, dtype=F32)
    inv = jnp.power(ROPE_THETA, -jnp.arange(0, ROPE_DIM, 2, dtype=F32) / ROPE_DIM)
    ang = pos[:, None] * inv[None, :]
    cos, sin = jnp.cos(ang), jnp.sin(ang)
    half = ROPE_DIM // 2
    gap = HEAD_DIM // 2 - half
    ones, zeros = jnp.ones((s, gap), F32), jnp.zeros((s, gap), F32)
    rope_cos = jnp.concatenate([cos, ones, cos, ones], axis=1)
    rope_sin = jnp.concatenate([-sin, zeros, sin, zeros], axis=1)
    return rope_cos, rope_sin


def _lane_runs(order):
    runs, start = [], 0
    for i in range(1, len(order) + 1):
        if i == len(order) or order[i] != order[i - 1] + 1:
            runs.append((start, order[start], i - start))
            start = i
    return runs


def _prep_qkv_kernel(w_ref, o_ref):
    j = pl.program_id(1)

    @pl.when(j == 2)
    def _():
        o_ref[...] = w_ref[...].astype(BF16)

    @pl.when(j < 2)
    def _():
        lane = lax.broadcasted_iota(jnp.int32, (w_ref.shape[0], HEAD_DIM), 1)
        runs = _lane_runs(_rotary_lane_order())
        for head in range(N_HEADS):
            cols = slice(head * HEAD_DIM, (head + 1) * HEAD_DIM)
            x = w_ref[:, cols]
            if head >= N_HEADS_NA:
                y = x
                for dst, src, length in runs:
                    if dst != src:
                        moved = pltpu.roll(x, (dst - src) % HEAD_DIM, 1)
                        y = jnp.where(jnp.logical_and(lane >= dst, lane < dst + length), moved, y)
                x = y
            o_ref[:, cols] = x.astype(BF16)


def _prep_qkv_weight(w_qkv):
    d, n = w_qkv.shape
    assert n == 3 * QKV_TN
    block = pl.BlockSpec((PREP_ROWS, QKV_TN), lambda i, j: (i, j))
    return pl.pallas_call(
        _prep_qkv_kernel,
        grid=(d // PREP_ROWS, 3),
        in_specs=[block],
        out_specs=block,
        out_shape=jax.ShapeDtypeStruct((d, n), BF16),
        compiler_params=_params(("parallel", "arbitrary"), 24),
        name="prep_qkv_weight",
    )(w_qkv)


def _na_window_start(ib, rows):
    return jnp.clip(NA_RB * ib - NA_ROWS // 2, 0, rows - NA_KR)


def _na_bias_kernel(rpb_ref, o_ref, *, rows):
    h = pl.program_id(0)
    n_ro = 2 * NA_ROWS - 1
    n_co = 2 * NA_COLS - 1
    c = lax.broadcasted_iota(jnp.int32, (GRID_W, 2 * GRID_W), 0)
    lane = lax.broadcasted_iota(jnp.int32, (GRID_W, 2 * GRID_W), 1)
    kc = lane & (GRID_W - 1)
    diff = kc - c + (NA_COLS - 1)
    cs = jnp.clip(c - NA_COLS // 2, 0, GRID_W - NA_COLS)
    col_ok = jnp.logical_and(kc >= cs, kc < cs + NA_COLS)
    neg = jnp.full((GRID_W, 2 * GRID_W), NEG, F32)
    toeplitz = []
    for ro in range(n_ro):
        t = jnp.zeros((GRID_W, 2 * GRID_W), F32)
        for m in range(n_co):
            t = jnp.where(diff == m, rpb_ref[(h * n_ro + ro) * n_co + m], t)
        toeplitz.append(jnp.where(col_ok, t * LOG2_E, neg))
    low_half = lane < GRID_W
    n_blocks = rows // NA_RB
    for var, ib in enumerate((0, n_blocks // 2, n_blocks - 1)):
        ws = min(max(NA_RB * ib - NA_ROWS // 2, 0), rows - NA_KR)
        for qr in range(NA_RB):
            r = NA_RB * ib + qr
            r_start = min(max(r - NA_ROWS // 2, 0), rows - NA_ROWS)
            for g in range(NA_KR // 2):
                pieces = []
                for kr in (2 * g, 2 * g + 1):
                    ka = ws + kr
                    inside = r_start <= ka < r_start + NA_ROWS
                    pieces.append(toeplitz[ka - r + NA_ROWS - 1] if inside else neg)
                o_ref[0, var, qr * GRID_W:(qr + 1) * GRID_W, g * 2 * GRID_W:(g + 1) * 2 * GRID_W] = (
                    jnp.where(low_half, pieces[0], pieces[1]))


def _na_bias(rpb, rows):
    nh = rpb.shape[0]
    return pl.pallas_call(
        functools.partial(_na_bias_kernel, rows=rows),
        grid=(nh,),
        in_specs=[pl.BlockSpec(memory_space=pltpu.SMEM)],
        out_specs=pl.BlockSpec((1, 3, NA_RB * GRID_W, NA_KR * GRID_W), lambda h: (h, 0, 0, 0)),
        out_shape=jax.ShapeDtypeStruct((nh, 3, NA_RB * GRID_W, NA_KR * GRID_W), F32),
        compiler_params=_params(("arbitrary",), 24),
        name="na_bias",
    )(rpb.reshape(-1))


def _softmax_pv(s2, v):
    m = jnp.max(s2, axis=-1, keepdims=True)
    p = jnp.exp2(s2 - m)
    l = jnp.sum(p, axis=-1, keepdims=True)
    return jnp.dot(p.astype(BF16), v, preferred_element_type=F32) / l


def _side_cast(refs):
    n = len(refs) // 2
    for src_ref, dst_ref in zip(refs[:n], refs[n:]):
        for c0 in range(0, src_ref.shape[1], SIDE_CAST_COLS):
            cols = slice(c0, c0 + SIDE_CAST_COLS)
            dst_ref[:, cols] = src_ref[:, cols].astype(BF16)


def _side_cast_plumbing(weights, n_steps, step_of):
    specs, shapes = [], []
    for w in weights:
        slab = w.shape[0] // n_steps
        assert slab * n_steps == w.shape[0] and w.shape[1] % SIDE_CAST_COLS == 0
        specs.append(pl.BlockSpec((slab, w.shape[1]), lambda *g: (step_of(*g), 0)))
        shapes.append(jax.ShapeDtypeStruct(w.shape, BF16))
    return specs, shapes


def _na_kernel(q_ref, k_ref, v_ref, bias_ref, *rest, rows):
    n_side = (len(rest) - 1) // 2
    o_ref = rest[n_side]
    _side_cast(rest[:n_side] + rest[n_side + 1:])
    n_blocks = rows // NA_RB
    tq = NA_RB * GRID_W

    def block(ib):
        start = pl.multiple_of(_na_window_start(ib, rows) * GRID_W, GRID_W)
        variant = jnp.where(ib == 0, 0, jnp.where(ib == n_blocks - 1, 2, 1))
        qrows = pl.ds(pl.multiple_of(ib * tq, tq), tq)
        k = k_ref[0, pl.ds(start, NA_KR * GRID_W), :]
        v = v_ref[0, pl.ds(start, NA_KR * GRID_W), :]
        s = lax.dot_general(q_ref[0, qrows, :], k, (((1,), (1,)), ((), ())),
                            preferred_element_type=F32)
        o_ref[0, qrows, :] = _softmax_pv(s + bias_ref[0, variant], v)

    def body(i, carry):
        for u in range(ATTN_UNROLL):
            block(i * ATTN_UNROLL + u)
        return carry

    lax.fori_loop(0, n_blocks // ATTN_UNROLL, body, 0)


def _na_attention(qkv, bias, d_model, side_weights=()):
    b, _, s, _ = qkv.shape
    rows = s // GRID_W
    heads_total = d_model // HEAD_DIM
    tq = NA_RB * GRID_W
    side_specs, side_shapes = _side_cast_plumbing(
        side_weights, b * N_HEADS_NA, lambda bi, h: bi * N_HEADS_NA + h)
    head_slab = lambda first: pl.BlockSpec((None, 1, s, HEAD_DIM), lambda bi, h: (bi, first + h, 0, 0))
    return pl.pallas_call(
        functools.partial(_na_kernel, rows=rows),
        grid=(b, N_HEADS_NA),
        in_specs=[
            head_slab(0), head_slab(heads_total), head_slab(2 * heads_total),
            pl.BlockSpec((1, 3, tq, NA_KR * GRID_W), lambda bi, h: (h, 0, 0, 0)),
        ] + side_specs,
        out_specs=[pl.BlockSpec((1, s, HEAD_DIM), lambda bi, h: (bi, 0, h))] + side_specs,
        out_shape=[jax.ShapeDtypeStruct((b, s, N_HEADS_NA * HEAD_DIM), F32)] + side_shapes,
        compiler_params=_params(("parallel", "arbitrary"), 52),
        name="na_attn",
    )(qkv, qkv, qkv, bias, *side_weights)


def _dil_bias_table():
    q = np.arange(DIL_TQ)[:, None]
    x = np.arange(DIL_WIN + 2 * DIL_SPAN)[None, :]
    d = x - 2 * DIL_SPAN - q
    mult = np.zeros(d.shape, np.int64)
    for window, dil in DIL_PATTERNS:
        reach = (window // 2 // dil) * dil
        mult += ((d % dil == 0) & (np.abs(d) <= reach)).astype(np.int64)
    table = np.where(mult > 0, np.log2(np.maximum(mult, 1)), NEG)
    return jnp.asarray(table, F32)


def _dil_kernel(q_ref, k_ref, v_ref, t_ref, *rest, seq):
    n_side = (len(rest) - 1) // 2
    o_ref = rest[n_side]
    _side_cast(rest[:n_side] + rest[n_side + 1:])

    def block(ib):
        t0 = ib * DIL_TQ
        ws = jnp.clip(t0 - DIL_SPAN, 0, seq - DIL_WIN)
        off = pl.multiple_of(ws - t0 + 2 * DIL_SPAN, DIL_TQ)
        ws = pl.multiple_of(ws, DIL_TQ)
        qrows = pl.ds(pl.multiple_of(t0, DIL_TQ), DIL_TQ)
        k = k_ref[0, pl.ds(ws, DIL_WIN), :]
        v = v_ref[0, pl.ds(ws, DIL_WIN), :]
        s = lax.dot_general(q_ref[0, qrows, :], k, (((1,), (1,)), ((), ())),
                            preferred_element_type=F32)
        o_ref[0, qrows, :] = _softmax_pv(s + t_ref[:, pl.ds(off, DIL_WIN)], v)

    def body(i, carry):
        for u in range(ATTN_UNROLL):
            block(i * ATTN_UNROLL + u)
        return carry

    lax.fori_loop(0, seq // DIL_TQ // ATTN_UNROLL, body, 0)


def _dil_attention(qkv, table, d_model, side_weights=()):
    b, _, s, _ = qkv.shape
    heads_total = d_model // HEAD_DIM
    h0 = N_HEADS_NA
    side_specs, side_shapes = _side_cast_plumbing(
        side_weights, b * N_HEADS_DIL, lambda bi, h: bi * N_HEADS_DIL + h)
    head_slab = lambda first: pl.BlockSpec((None, 1, s, HEAD_DIM), lambda bi, h: (bi, first + h, 0, 0))
    return pl.pallas_call(
        functools.partial(_dil_kernel, seq=s),
        grid=(b, N_HEADS_DIL),
        in_specs=[
            head_slab(h0), head_slab(heads_total + h0), head_slab(2 * heads_total + h0),
            pl.BlockSpec(table.shape, lambda bi, h: (0, 0)),
        ] + side_specs,
        out_specs=[pl.BlockSpec((1, s, HEAD_DIM), lambda bi, h: (bi, 0, h))] + side_specs,
        out_shape=[jax.ShapeDtypeStruct((b, s, N_HEADS_DIL * HEAD_DIM), F32)] + side_shapes,
        compiler_params=_params(("parallel", "arbitrary"), 52),
        name="dil_attn",
    )(qkv, qkv, qkv, table, *side_weights)


def _oproj_kernel(ona_ref, odil_ref, h_ref, gna_ref, gdil_ref, gt_ref, w_ref,
                  gn_ref, shn_ref, scn_ref, o_ref, xnn_ref):
    d_na = ona_ref.shape[2]
    na = _rms(ona_ref[0], gna_ref[...]).astype(BF16)
    nd = _rms(odil_ref[0], gdil_ref[...]).astype(BF16)
    mix = jnp.dot(na, w_ref[:d_na, :], preferred_element_type=F32)
    mix = mix + jnp.dot(nd, w_ref[d_na:, :], preferred_element_type=F32)
    o_ref[0] = h_ref[0] + gt_ref[0] * mix
    _norm_modulate_rows(o_ref.at[0], gn_ref, shn_ref, scn_ref, xnn_ref.at[0])


def _oproj(o_na, o_dil, h, g_na, g_dil, gate, w_o, next_norm):
    b, s, d = h.shape
    d_na, d_dil = o_na.shape[2], o_dil.shape[2]
    gain, shift, scale = next_norm
    tile = pl.BlockSpec((1, OPROJ_TM, d), lambda bi, i: (bi, i, 0))
    vec = pl.BlockSpec((1, 1, d), lambda bi, i: (bi, 0, 0))
    return pl.pallas_call(
        _oproj_kernel,
        grid=(b, s // OPROJ_TM),
        in_specs=[
            pl.BlockSpec((1, OPROJ_TM, d_na), lambda bi, i: (bi, i, 0)),
            pl.BlockSpec((1, OPROJ_TM, d_dil), lambda bi, i: (bi, i, 0)),
            tile,
            pl.BlockSpec((1, d_na), lambda bi, i: (0, 0)),
            pl.BlockSpec((1, d_dil), lambda bi, i: (0, 0)),
            vec,
            pl.BlockSpec((d, d), lambda bi, i: (0, 0)),
            pl.BlockSpec((1, d), lambda bi, i: (0, 0)), vec, vec,
        ],
        out_specs=[tile, tile],
        out_shape=[jax.ShapeDtypeStruct((b, s, d), F32), jax.ShapeDtypeStruct((b, s, d), BF16)],
        compiler_params=_params(("parallel", "parallel"), 52),
        name="oproj",
    )(o_na, o_dil, h, g_na.reshape(1, d_na), g_dil.reshape(1, d_dil), gate, w_o,
      gain.reshape(1, d), shift, scale)


def kernel(x, c, w_ada, b_ada, g_ffn1, w1_gate, w1_up, w1_down, g_mix, w_qkv, qn_na, kn_na, qn_dil, kn_dil, rpb_na, g_out_na, g_out_dil, w_o, g_ffn2, w2_gate, w2_up, w2_down):
    b, s, d = x.shape
    depth = w_ada.shape[0]
    rows = s // GRID_W
    rope_cos, rope_sin = _rope_tables(s)
    dil_table = _dil_bias_table()
    c_pad = jnp.pad(c, ((0, 8 - b), (0, 0)))
    lane_order = _rotary_lane_order()
    q_factor = ATTN_SCALE * LOG2_E

    h = x
    for l in range(depth):
        mod = _ada(c_pad, w_ada[l], b_ada[l])[:b]
        sh1, sc1, gt1, sh2, sc2, gt2, sh3, sc3, gt3 = [
            m.reshape(b, 1, d) for m in jnp.split(mod, N_MOD, axis=-1)]

        xn = _norm_mod(h, g_ffn1[l], sh1, sc1)
        h, xn = _ffn(xn, h, gt1,
                     w1_gate[l].astype(BF16), w1_up[l].astype(BF16), w1_down[l].astype(BF16),
                     next_norm=(g_mix[l], sh2, sc2))

        head_gains = jnp.stack([
            jnp.stack([qn_na[l] * q_factor, _take_lanes(qn_dil[l], lane_order) * q_factor]),
            jnp.stack([kn_na[l], _take_lanes(kn_dil[l], lane_order)]),
        ]).reshape(2, 2, 1, HEAD_DIM)
        in_hbm = lambda a: pltpu.with_memory_space_constraint(a, pltpu.HBM)
        qkv = _qkv(xn, in_hbm(_prep_qkv_weight(w_qkv[l])), head_gains, rope_cos, rope_sin)

        o_na, w2g, w2u, wo = _na_attention(qkv, in_hbm(_na_bias(rpb_na[l], rows)), d,
                                           side_weights=(w2_gate[l], w2_up[l], w_o[l]))
        o_dil, w2d = _dil_attention(qkv, dil_table, d, side_weights=(w2_down[l],))
        h, xn = _oproj(o_na, o_dil, h, g_out_na[l], g_out_dil[l], gt2, wo,
                       next_norm=(g_ffn2[l], sh3, sc3))

        h = _ffn(xn, h, gt3, w2g, w2u, w2d)
    return h
```

```python
import functools
import math

import numpy as np
import jax
import jax.numpy as jnp
from jax import lax
from jax.experimental import pallas as pl
from jax.experimental.pallas import tpu as pltpu

HEAD_DIM = 128
N_HEADS = 16
N_HEADS_NA = 8
N_HEADS_DIL = N_HEADS - N_HEADS_NA
GRID_W = 64
NA_ROWS = 8
NA_COLS = 16
DIL_PATTERNS = ((128, 1), (512, 4), (2048, 16))
ROPE_THETA = 500000.0
ROPE_DIM = HEAD_DIM // 4
N_MOD = 9
EPS = 1e-6
NEG = -1e30
ATTN_SCALE = HEAD_DIM ** -0.5
LOG2_E = math.log2(math.e)

BF16 = jnp.bfloat16
F32 = jnp.float32

MIB = 1024 * 1024

ADA_TN = 1024
NORM_TM = 512
NORM_ROWS = 32
NORM_UNROLL = 4
FFN_TM = 1024
FFN_TF = 512
FFN_RES = 256
QKV_TM = 512
QKV_TN = 2048
QKV_PIECE = 256
SIDE_CAST_COLS = 512
PREP_ROWS = 512
OPROJ_TM = 512
ATTN_UNROLL = 4
NA_RB = 4
NA_KR = 12
DIL_TQ = 256
DIL_NEAR_PATTERNS = DIL_PATTERNS[:-1]
DIL_FAR = DIL_PATTERNS[-1][1]
DIL_FAR_STEPS = DIL_PATTERNS[-1][0] // 2 // DIL_FAR
DIL_SPAN = max(w // 2 // d * d for w, d in DIL_NEAR_PATTERNS)
DIL_WIN = DIL_TQ + 2 * DIL_SPAN


def _params(sem, vmem_mib):
    return pltpu.CompilerParams(dimension_semantics=sem, vmem_limit_bytes=vmem_mib * MIB)


def _silu(x):
    return x * (1.0 / (1.0 + jnp.exp(-x)))


def _rms(x, gain):
    ms = jnp.mean(x * x, axis=-1, keepdims=True)
    return x * lax.rsqrt(ms + EPS) * gain


def _norm_modulate_rows(x_ref, g_ref, sh_ref, sc_ref, xn_ref):
    n_rows = xn_ref.shape[0]
    gain_scale = g_ref[...] * (1.0 + sc_ref[0])
    shift = sh_ref[0]

    def body(i, carry):
        rows = pl.ds(pl.multiple_of(i * NORM_ROWS, NORM_ROWS), NORM_ROWS)
        xn_ref[rows, :] = (_rms(x_ref[rows, :], gain_scale) + shift).astype(BF16)
        return carry

    lax.fori_loop(0, n_rows // NORM_ROWS, body, 0, unroll=NORM_UNROLL)


def _norm_mod_kernel(x_ref, g_ref, sh_ref, sc_ref, xn_ref):
    _norm_modulate_rows(x_ref.at[0], g_ref, sh_ref, sc_ref, xn_ref.at[0])


def _norm_mod(x, gain, shift, scale):
    b, s, d = x.shape
    tile = pl.BlockSpec((1, NORM_TM, d), lambda bi, i: (bi, i, 0))
    vec = pl.BlockSpec((1, 1, d), lambda bi, i: (bi, 0, 0))
    return pl.pallas_call(
        _norm_mod_kernel,
        grid=(b, s // NORM_TM),
        in_specs=[tile, pl.BlockSpec((1, d), lambda bi, i: (0, 0)), vec, vec],
        out_specs=tile,
        out_shape=jax.ShapeDtypeStruct((b, s, d), BF16),
        compiler_params=_params(("parallel", "parallel"), 24),
        name="norm_mod",
    )(x, gain.reshape(1, d), shift, scale)


def _ada_kernel(c_ref, w_ref, b_ref, o_ref):
    a = _silu(c_ref[...]).astype(BF16)
    w = w_ref[...].astype(BF16)
    o_ref[...] = jnp.dot(a, w, preferred_element_type=F32) + b_ref[...]


def _ada(c_pad, w_ada, b_ada):
    rows, d = c_pad.shape
    n = w_ada.shape[1]
    return pl.pallas_call(
        _ada_kernel,
        grid=(n // ADA_TN,),
        in_specs=[
            pl.BlockSpec((rows, d), lambda j: (0, 0)),
            pl.BlockSpec((d, ADA_TN), lambda j: (0, j)),
            pl.BlockSpec((1, ADA_TN), lambda j: (0, j)),
        ],
        out_specs=pl.BlockSpec((rows, ADA_TN), lambda j: (0, j)),
        out_shape=jax.ShapeDtypeStruct((rows, n), F32),
        compiler_params=_params(("arbitrary",), 24),
        name="ada_mod",
    )(c_pad, w_ada, b_ada.reshape(1, n))


def _ffn_kernel(xn_ref, xres_ref, gt_ref, wg_ref, wu_ref, wd_ref, *rest, emit_next):
    if emit_next:
        gn_ref, shn_ref, scn_ref, o_ref, xnn_ref = rest
    else:
        (o_ref,) = rest
    j = pl.program_id(2)
    n_res = o_ref.shape[2] // FFN_RES

    @pl.when(j == 0)
    def _():
        o_ref[0] = jnp.zeros(o_ref.shape[1:], F32)

    xn = xn_ref[0]
    g = jnp.dot(xn, wg_ref[...], preferred_element_type=F32)
    u = jnp.dot(xn, wu_ref[...], preferred_element_type=F32)
    a = (_silu(g) * u).astype(BF16)
    o_ref[0] += (0.5 * gt_ref[0]) * jnp.dot(a, wd_ref[...], preferred_element_type=F32)

    @pl.when(j < n_res)
    def _():
        cols = pl.ds(pl.multiple_of(j * FFN_RES, FFN_RES), FFN_RES)
        o_ref[0, :, cols] += xres_ref[0]

    if emit_next:
        @pl.when(j == pl.num_programs(2) - 1)
        def _():
            _norm_modulate_rows(o_ref.at[0], gn_ref, shn_ref, scn_ref, xnn_ref.at[0])


def _ffn(xn, x, gate, w_gate, w_up, w_down, next_norm=None):
    b, s, d = x.shape
    dff = w_gate.shape[1]
    n_steps = dff // FFN_TF
    n_res = d // FFN_RES
    assert n_res <= n_steps
    tile = pl.BlockSpec((1, FFN_TM, d), lambda bi, i, j: (bi, i, 0))
    vec = pl.BlockSpec((1, 1, d), lambda bi, i, j: (bi, 0, 0))
    in_specs = [
        tile,
        pl.BlockSpec((1, FFN_TM, FFN_RES), lambda bi, i, j: (bi, i, jnp.minimum(j, n_res - 1))),
        vec,
        pl.BlockSpec((d, FFN_TF), lambda bi, i, j: (0, j)),
        pl.BlockSpec((d, FFN_TF), lambda bi, i, j: (0, j)),
        pl.BlockSpec((FFN_TF, d), lambda bi, i, j: (j, 0)),
    ]
    args = [xn, x, gate, w_gate, w_up, w_down]
    out_specs = [tile]
    out_shape = [jax.ShapeDtypeStruct((b, s, d), F32)]
    if next_norm is not None:
        gain, shift, scale = next_norm
        in_specs += [pl.BlockSpec((1, d), lambda bi, i, j: (0, 0)), vec, vec]
        args += [gain.reshape(1, d), shift, scale]
        out_specs.append(tile)
        out_shape.append(jax.ShapeDtypeStruct((b, s, d), BF16))
    outs = pl.pallas_call(
        functools.partial(_ffn_kernel, emit_next=next_norm is not None),
        grid=(b, s // FFN_TM, n_steps),
        in_specs=in_specs,
        out_specs=out_specs,
        out_shape=out_shape,
        compiler_params=_params(("parallel", "parallel", "arbitrary"), 56),
        name="ffn",
    )(*args)
    return outs if next_norm is not None else outs[0]


def _qkv_kernel(xn_ref, w_ref, hg_ref, cos_ref, sin_ref, o_ref, o_far_ref, ybuf_ref):
    j = pl.program_id(2)
    n_pieces = QKV_TN // QKV_PIECE
    heads_per_piece = QKV_PIECE // HEAD_DIM
    rows_far = o_far_ref.shape[2]

    def store(head, y):
        o_ref[head] = y.astype(BF16)
        if head >= N_HEADS_NA:
            hd = head - N_HEADS_NA
            ybuf_ref[hd] = y
            for r in range(DIL_FAR):
                o_far_ref[hd, r] = ybuf_ref[hd, pl.ds(r, rows_far, stride=DIL_FAR), :].astype(BF16)

    def normed(r, head):
        return _rms(r, hg_ref[0, head // N_HEADS_NA])

    def normed_rotary(r, head):
        y = normed(r, head)
        return y * cos_ref[...] + pltpu.roll(y, HEAD_DIM // 2, 1) * sin_ref[...]

    def pieces(epilogue_of_head):
        xn = xn_ref[0]
        for p in list(range(n_pieces // 2, n_pieces)) + list(range(n_pieces // 2)):
            r = jnp.dot(xn, w_ref[:, p * QKV_PIECE:(p + 1) * QKV_PIECE],
                        preferred_element_type=F32)
            for hh in range(heads_per_piece):
                head = p * heads_per_piece + hh
                store(head, epilogue_of_head(head)(r[:, hh * HEAD_DIM:(hh + 1) * HEAD_DIM], head))

    @pl.when(j < 2)
    def _():
        pieces(lambda head: normed if head < N_HEADS_NA else normed_rotary)

    @pl.when(j == 2)
    def _():
        pieces(lambda head: (lambda r, _: r))


def _qkv(xn, w_qkv, head_gains, rope_cos, rope_sin):
    b, s, d = xn.shape
    n = w_qkv.shape[1]
    assert QKV_TN == N_HEADS * HEAD_DIM and n == 3 * QKV_TN
    rope = pl.BlockSpec((QKV_TM, HEAD_DIM), lambda bi, i, j: (i, 0))
    return pl.pallas_call(
        _qkv_kernel,
        grid=(b, s // QKV_TM, n // QKV_TN),
        in_specs=[
            pl.BlockSpec((1, QKV_TM, d), lambda bi, i, j: (bi, i, 0)),
            pl.BlockSpec((d, QKV_TN), lambda bi, i, j: (0, j)),
            pl.BlockSpec((1, 2, 1, HEAD_DIM), lambda bi, i, j: (jnp.minimum(j, 1), 0, 0, 0)),
            rope, rope,
        ],
        out_specs=[
            pl.BlockSpec((None, N_HEADS, QKV_TM, HEAD_DIM), lambda bi, i, j: (bi, j, i, 0)),
            pl.BlockSpec((None, N_HEADS_DIL, DIL_FAR, QKV_TM // DIL_FAR, HEAD_DIM),
                         lambda bi, i, j: (bi, j, 0, i, 0)),
        ],
        out_shape=[
            jax.ShapeDtypeStruct((b, 3 * N_HEADS, s, HEAD_DIM), BF16),
            jax.ShapeDtypeStruct((b, 3 * N_HEADS_DIL, DIL_FAR, s // DIL_FAR, HEAD_DIM), BF16),
        ],
        scratch_shapes=[pltpu.VMEM((N_HEADS_DIL, QKV_TM, HEAD_DIM), F32)],
        compiler_params=_params(("parallel", "parallel", "arbitrary"), 48),
        name="qkv",
    )(xn, w_qkv, head_gains, rope_cos, rope_sin)


def _rotary_lane_order():
    half = ROPE_DIM // 2
    mid = HEAD_DIM // 2
    order = (list(range(half)) + list(range(ROPE_DIM, ROPE_DIM + mid - half))
             + list(range(half, ROPE_DIM)) + list(range(ROPE_DIM + mid - half, HEAD_DIM)))
    assert sorted(order) == list(range(HEAD_DIM))
    return order


def _take_lanes(a, order):
    runs, start = [], 0
    for i in range(1, len(order) + 1):
        if i == len(order) or order[i] != order[i - 1] + 1:
            runs.append(a[..., order[start]:order[i - 1] + 1])
            start = i
    return jnp.concatenate(runs, axis=-1)


def _rope_tables(s):
    pos = jnp.arange(s, dtype=F32)
    inv = jnp.power(ROPE_THETA, -jnp.arange(0, ROPE_DIM, 2, dtype=F32) / ROPE_DIM)
    ang = pos[:, None] * inv[None, :]
    cos, sin = jnp.cos(ang), jnp.sin(ang)
    half = ROPE_DIM // 2
    gap = HEAD_DIM // 2 - half
    ones, zeros = jnp.ones((s, gap), F32), jnp.zeros((s, gap), F32)
    rope_cos = jnp.concatenate([cos, ones, cos, ones], axis=1)
    rope_sin = jnp.concatenate([-sin, zeros, sin, zeros], axis=1)
    return rope_cos, rope_sin


def _lane_runs(order):
    runs, start = [], 0
    for i in range(1, len(order) + 1):
        if i == len(order) or order[i] != order[i - 1] + 1:
            runs.append((start, order[start], i - start))
            start = i
    return runs


def _prep_qkv_kernel(w_ref, o_ref):
    j = pl.program_id(1)

    @pl.when(j == 2)
    def _():
        o_ref[...] = w_ref[...].astype(BF16)

    @pl.when(j < 2)
    def _():
        lane = lax.broadcasted_iota(jnp.int32, (w_ref.shape[0], HEAD_DIM), 1)
        runs = _lane_runs(_rotary_lane_order())
        for head in range(N_HEADS):
            cols = slice(head * HEAD_DIM, (head + 1) * HEAD_DIM)
            x = w_ref[:, cols]
            if head >= N_HEADS_NA:
                y = x
                for dst, src, length in runs:
                    if dst != src:
                        moved = pltpu.roll(x, (dst - src) % HEAD_DIM, 1)
                        y = jnp.where(jnp.logical_and(lane >= dst, lane < dst + length), moved, y)
                x = y
            o_ref[:, cols] = x.astype(BF16)


def _prep_qkv_weight(w_qkv):
    d, n = w_qkv.shape
    assert n == 3 * QKV_TN
    block = pl.BlockSpec((PREP_ROWS, QKV_TN), lambda i, j: (i, j))
    return pl.pallas_call(
        _prep_qkv_kernel,
        grid=(d // PREP_ROWS, 3),
        in_specs=[block],
        out_specs=block,
        out_shape=jax.ShapeDtypeStruct((d, n), BF16),
        compiler_params=_params(("parallel", "arbitrary"), 24),
        name="prep_qkv_weight",
    )(w_qkv)


def _na_window_start(ib, rows):
    return jnp.clip(NA_RB * ib - NA_ROWS // 2, 0, rows - NA_KR)


def _na_bias_kernel(rpb_ref, o_ref, *, rows):
    h = pl.program_id(0)
    n_ro = 2 * NA_ROWS - 1
    n_co = 2 * NA_COLS - 1
    c = lax.broadcasted_iota(jnp.int32, (GRID_W, 2 * GRID_W), 0)
    lane = lax.broadcasted_iota(jnp.int32, (GRID_W, 2 * GRID_W), 1)
    kc = lane & (GRID_W - 1)
    diff = kc - c + (NA_COLS - 1)
    cs = jnp.clip(c - NA_COLS // 2, 0, GRID_W - NA_COLS)
    col_ok = jnp.logical_and(kc >= cs, kc < cs + NA_COLS)
    neg = jnp.full((GRID_W, 2 * GRID_W), NEG, F32)
    toeplitz = []
    for ro in range(n_ro):
        t = jnp.zeros((GRID_W, 2 * GRID_W), F32)
        for m in range(n_co):
            t = jnp.where(diff == m, rpb_ref[(h * n_ro + ro) * n_co + m], t)
        toeplitz.append(jnp.where(col_ok, t * LOG2_E, neg))
    low_half = lane < GRID_W
    n_blocks = rows // NA_RB
    for var, ib in enumerate((0, n_blocks // 2, n_blocks - 1)):
        ws = min(max(NA_RB * ib - NA_ROWS // 2, 0), rows - NA_KR)
        for qr in range(NA_RB):
            r = NA_RB * ib + qr
            r_start = min(max(r - NA_ROWS // 2, 0), rows - NA_ROWS)
            for g in range(NA_KR // 2):
                pieces = []
                for kr in (2 * g, 2 * g + 1):
                    ka = ws + kr
                    inside = r_start <= ka < r_start + NA_ROWS
                    pieces.append(toeplitz[ka - r + NA_ROWS - 1] if inside else neg)
                o_ref[0, var, qr * GRID_W:(qr + 1) * GRID_W, g * 2 * GRID_W:(g + 1) * 2 * GRID_W] = (
                    jnp.where(low_half, pieces[0], pieces[1]))


def _na_bias(rpb, rows):
    nh = rpb.shape[0]
    return pl.pallas_call(
        functools.partial(_na_bias_kernel, rows=rows),
        grid=(nh,),
        in_specs=[pl.BlockSpec(memory_space=pltpu.SMEM)],
        out_specs=pl.BlockSpec((1, 3, NA_RB * GRID_W, NA_KR * GRID_W), lambda h: (h, 0, 0, 0)),
        out_shape=jax.ShapeDtypeStruct((nh, 3, NA_RB * GRID_W, NA_KR * GRID_W), F32),
        compiler_params=_params(("arbitrary",), 24),
        name="na_bias",
    )(rpb.reshape(-1))


def _softmax_pv(s2, v):
    m = jnp.max(s2, axis=-1, keepdims=True)
    p = jnp.exp2(s2 - m)
    l = jnp.sum(p, axis=-1, keepdims=True)
    return jnp.dot(p.astype(BF16), v, preferred_element_type=F32) / l


def _side_cast(refs):
    n = len(refs) // 2
    for src_ref, dst_ref in zip(refs[:n], refs[n:]):
        for c0 in range(0, src_ref.shape[1], SIDE_CAST_COLS):
            cols = slice(c0, c0 + SIDE_CAST_COLS)
            dst_ref[:, cols] = src_ref[:, cols].astype(BF16)


def _side_cast_plumbing(weights, n_steps, step_of):
    specs, shapes = [], []
    for w in weights:
        slab = w.shape[0] // n_steps
        assert slab * n_steps == w.shape[0] and w.shape[1] % SIDE_CAST_COLS == 0
        specs.append(pl.BlockSpec((slab, w.shape[1]), lambda *g: (step_of(*g), 0)))
        shapes.append(jax.ShapeDtypeStruct(w.shape, BF16))
    return specs, shapes


def _na_kernel(q_ref, k_ref, v_ref, bias_ref, *rest, rows):
    n_side = (len(rest) - 1) // 2
    o_ref = rest[n_side]
    _side_cast(rest[:n_side] + rest[n_side + 1:])
    n_blocks = rows // NA_RB
    tq = NA_RB * GRID_W

    def block(ib):
        start = pl.multiple_of(_na_window_start(ib, rows) * GRID_W, GRID_W)
        variant = jnp.where(ib == 0, 0, jnp.where(ib == n_blocks - 1, 2, 1))
        qrows = pl.ds(pl.multiple_of(ib * tq, tq), tq)
        k = k_ref[0, pl.ds(start, NA_KR * GRID_W), :]
        v = v_ref[0, pl.ds(start, NA_KR * GRID_W), :]
        s = lax.dot_general(q_ref[0, qrows, :], k, (((1,), (1,)), ((), ())),
                            preferred_element_type=F32)
        o_ref[0, qrows, :] = _softmax_pv(s + bias_ref[0, variant], v)

    def body(i, carry):
        for u in range(ATTN_UNROLL):
            block(i * ATTN_UNROLL + u)
        return carry

    lax.fori_loop(0, n_blocks // ATTN_UNROLL, body, 0)


def _na_attention(qkv, bias, d_model, side_weights=()):
    b, _, s, _ = qkv.shape
    rows = s // GRID_W
    heads_total = d_model // HEAD_DIM
    tq = NA_RB * GRID_W
    side_specs, side_shapes = _side_cast_plumbing(
        side_weights, b * N_HEADS_NA, lambda bi, h: bi * N_HEADS_NA + h)
    head_slab = lambda first: pl.BlockSpec((None, 1, s, HEAD_DIM), lambda bi, h: (bi, first + h, 0, 0))
    return pl.pallas_call(
        functools.partial(_na_kernel, rows=rows),
        grid=(b, N_HEADS_NA),
        in_specs=[
            head_slab(0), head_slab(heads_total), head_slab(2 * heads_total),
            pl.BlockSpec((1, 3, tq, NA_KR * GRID_W), lambda bi, h: (h, 0, 0, 0)),
        ] + side_specs,
        out_specs=[pl.BlockSpec((1, s, HEAD_DIM), lambda bi, h: (bi, 0, h))] + side_specs,
        out_shape=[jax.ShapeDtypeStruct((b, s, N_HEADS_NA * HEAD_DIM), F32)] + side_shapes,
        compiler_params=_params(("parallel", "arbitrary"), 52),
        name="na_attn",
    )(qkv, qkv, qkv, bias, *side_weights)


def _dil_near_table():
    q = np.arange(DIL_TQ)[:, None]
    x = np.arange(DIL_WIN + 2 * DIL_SPAN)[None, :]
    d = x - 2 * DIL_SPAN - q
    mult = np.zeros(d.shape, np.int64)
    for window, dil in DIL_NEAR_PATTERNS:
        reach = (window // 2 // dil) * dil
        mult += ((d % dil == 0) & (np.abs(d) <= reach)).astype(np.int64)
    table = np.where(mult > 0, np.log2(np.maximum(mult, 1)), NEG)
    return jnp.asarray(table, F32)


def _dil_far_band(length):
    i = np.arange(length)
    inside = np.abs(i[:, None] - i[None, :]) <= DIL_FAR_STEPS
    return jnp.asarray(np.where(inside, 0.0, NEG), F32)


def _dil_kernel(q_ref, k_ref, v_ref, qf_ref, kf_ref, vf_ref, t_ref, band_ref, *rest, seq):
    ofar_ref, lsefar_ref = rest[-2:]
    rest = rest[:-2]
    n_side = (len(rest) - 1) // 2
    o_ref = rest[n_side]
    _side_cast(rest[:n_side] + rest[n_side + 1:])
    length = seq // DIL_FAR
    contract_last = (((1,), (1,)), ((), ()))

    def far_block(r):
        s = lax.dot_general(qf_ref[0, r], kf_ref[0, r], contract_last, preferred_element_type=F32)
        s = s + band_ref[...]
        m = jnp.max(s, axis=-1, keepdims=True)
        p = jnp.exp2(s - m)
        l = jnp.sum(p, axis=-1, keepdims=True)
        o = jnp.dot(p.astype(BF16), vf_ref[0, r], preferred_element_type=F32) / l
        rows = pl.ds(r, length, stride=DIL_FAR)
        ofar_ref[rows, :] = o
        lsefar_ref[rows, :] = jnp.broadcast_to(m + jnp.log2(l), o.shape)

    def near_block(ib):
        t0 = ib * DIL_TQ
        ws = jnp.clip(t0 - DIL_SPAN, 0, seq - DIL_WIN)
        off = pl.multiple_of(ws - t0 + 2 * DIL_SPAN, DIL_TQ)
        ws = pl.multiple_of(ws, DIL_TQ)
        qrows = pl.ds(pl.multiple_of(t0, DIL_TQ), DIL_TQ)
        k = k_ref[0, pl.ds(ws, DIL_WIN), :]
        v = v_ref[0, pl.ds(ws, DIL_WIN), :]
        s = lax.dot_general(q_ref[0, qrows, :], k, contract_last, preferred_element_type=F32)
        s = s + t_ref[:, pl.ds(off, DIL_WIN)]
        m = jnp.max(s, axis=-1, keepdims=True)
        p = jnp.exp2(s - m)
        l = jnp.sum(p, axis=-1, keepdims=True)
        acc = jnp.dot(p.astype(BF16), v, preferred_element_type=F32)
        lse_far = lsefar_ref[qrows, :]
        top = jnp.maximum(m, lse_far)
        w_near = jnp.exp2(m - top)
        w_far = jnp.exp2(lse_far - top)
        o_ref[0, qrows, :] = (w_near * acc + w_far * ofar_ref[qrows, :]) / (w_near * l + w_far)

    def unrolled(block):
        def body(i, carry):
            for u in range(ATTN_UNROLL):
                block(i * ATTN_UNROLL + u)
            return carry
        return body

    lax.fori_loop(0, DIL_FAR // ATTN_UNROLL, unrolled(far_block), 0)
    lax.fori_loop(0, seq // DIL_TQ // ATTN_UNROLL, unrolled(near_block), 0)


def _dil_attention(qkv, qkv_far, table, band, d_model, side_weights=()):
    b, _, s, _ = qkv.shape
    heads_total = d_model // HEAD_DIM
    h0 = N_HEADS_NA
    length = s // DIL_FAR
    assert band.shape == (length, length)
    side_specs, side_shapes = _side_cast_plumbing(
        side_weights, b * N_HEADS_DIL, lambda bi, h: bi * N_HEADS_DIL + h)
    head_slab = lambda first: pl.BlockSpec((None, 1, s, HEAD_DIM), lambda bi, h: (bi, first + h, 0, 0))
    far_slab = lambda first: pl.BlockSpec((None, 1, DIL_FAR, length, HEAD_DIM),
                                          lambda bi, h: (bi, first + h, 0, 0, 0))
    return pl.pallas_call(
        functools.partial(_dil_kernel, seq=s),
        grid=(b, N_HEADS_DIL),
        in_specs=[
            head_slab(h0), head_slab(heads_total + h0), head_slab(2 * heads_total + h0),
            far_slab(0), far_slab(N_HEADS_DIL), far_slab(2 * N_HEADS_DIL),
            pl.BlockSpec(table.shape, lambda bi, h: (0, 0)),
            pl.BlockSpec(band.shape, lambda bi, h: (0, 0)),
        ] + side_specs,
        out_specs=[pl.BlockSpec((1, s, HEAD_DIM), lambda bi, h: (bi, 0, h))] + side_specs,
        out_shape=[jax.ShapeDtypeStruct((b, s, N_HEADS_DIL * HEAD_DIM), F32)] + side_shapes,
        scratch_shapes=[pltpu.VMEM((s, HEAD_DIM), F32), pltpu.VMEM((s, HEAD_DIM), F32)],
        compiler_params=_params(("parallel", "arbitrary"), 52),
        name="dil_attn",
    )(qkv, qkv, qkv, qkv_far, qkv_far, qkv_far, table, band, *side_weights)


def _oproj_kernel(ona_ref, odil_ref, h_ref, gna_ref, gdil_ref, gt_ref, w_ref,
                  gn_ref, shn_ref, scn_ref, o_ref, xnn_ref):
    d_na = ona_ref.shape[2]
    na = _rms(ona_ref[0], gna_ref[...]).astype(BF16)
    nd = _rms(odil_ref[0], gdil_ref[...]).astype(BF16)
    mix = jnp.dot(na, w_ref[:d_na, :], preferred_element_type=F32)
    mix = mix + jnp.dot(nd, w_ref[d_na:, :], preferred_element_type=F32)
    o_ref[0] = h_ref[0] + gt_ref[0] * mix
    _norm_modulate_rows(o_ref.at[0], gn_ref, shn_ref, scn_ref, xnn_ref.at[0])


def _oproj(o_na, o_dil, h, g_na, g_dil, gate, w_o, next_norm):
    b, s, d = h.shape
    d_na, d_dil = o_na.shape[2], o_dil.shape[2]
    gain, shift, scale = next_norm
    tile = pl.BlockSpec((1, OPROJ_TM, d), lambda bi, i: (bi, i, 0))
    vec = pl.BlockSpec((1, 1, d), lambda bi, i: (bi, 0, 0))
    return pl.pallas_call(
        _oproj_kernel,
        grid=(b, s // OPROJ_TM),
        in_specs=[
            pl.BlockSpec((1, OPROJ_TM, d_na), lambda bi, i: (bi, i, 0)),
            pl.BlockSpec((1, OPROJ_TM, d_dil), lambda bi, i: (bi, i, 0)),
            tile,
            pl.BlockSpec((1, d_na), lambda bi, i: (0, 0)),
            pl.BlockSpec((1, d_dil), lambda bi, i: (0, 0)),
            vec,
            pl.BlockSpec((d, d), lambda bi, i: (0, 0)),
            pl.BlockSpec((1, d), lambda bi, i: (0, 0)), vec, vec,
        ],
        out_specs=[tile, tile],
        out_shape=[jax.ShapeDtypeStruct((b, s, d), F32), jax.ShapeDtypeStruct((b, s, d), BF16)],
        compiler_params=_params(("parallel", "parallel"), 52),
        name="oproj",
    )(o_na, o_dil, h, g_na.reshape(1, d_na), g_dil.reshape(1, d_dil), gate, w_o,
      gain.reshape(1, d), shift, scale)


def kernel(x, c, w_ada, b_ada, g_ffn1, w1_gate, w1_up, w1_down, g_mix, w_qkv, qn_na, kn_na, qn_dil, kn_dil, rpb_na, g_out_na, g_out_dil, w_o, g_ffn2, w2_gate, w2_up, w2_down):
    b, s, d = x.shape
    depth = w_ada.shape[0]
    rows = s // GRID_W
    rope_cos, rope_sin = _rope_tables(s)
    dil_table = _dil_near_table()
    dil_band = _dil_far_band(s // DIL_FAR)
    c_pad = jnp.pad(c, ((0, 8 - b), (0, 0)))
    lane_order = _rotary_lane_order()
    q_factor = ATTN_SCALE * LOG2_E

    h = x
    for l in range(depth):
        mod = _ada(c_pad, w_ada[l], b_ada[l])[:b]
        sh1, sc1, gt1, sh2, sc2, gt2, sh3, sc3, gt3 = [
            m.reshape(b, 1, d) for m in jnp.split(mod, N_MOD, axis=-1)]

        xn = _norm_mod(h, g_ffn1[l], sh1, sc1)
        h, xn = _ffn(xn, h, gt1,
                     w1_gate[l].astype(BF16), w1_up[l].astype(BF16), w1_down[l].astype(BF16),
                     next_norm=(g_mix[l], sh2, sc2))

        head_gains = jnp.stack([
            jnp.stack([qn_na[l] * q_factor, _take_lanes(qn_dil[l], lane_order) * q_factor]),
            jnp.stack([kn_na[l], _take_lanes(kn_dil[l], lane_order)]),
        ]).reshape(2, 2, 1, HEAD_DIM)
        qkv, qkv_far = _qkv(xn, _prep_qkv_weight(w_qkv[l]), head_gains, rope_cos, rope_sin)

        o_na, w2g, w2u, wo = _na_attention(qkv, _na_bias(rpb_na[l], rows), d,
                                           side_weights=(w2_gate[l], w2_up[l], w_o[l]))
        o_dil, w2d = _dil_attention(qkv, qkv_far, dil_table, dil_band, d,
                                    side_weights=(w2_down[l],))
        h, xn = _oproj(o_na, o_dil, h, g_out_na[l], g_out_dil[l], gt2, wo,
                       next_norm=(g_ffn2[l], sh3, sc3))

        h = _ffn(xn, h, gt3, w2g, w2u, w2d)
    return h
```

```python
import functools
import math

import numpy as np
import jax
import jax.numpy as jnp
from jax import lax
from jax.experimental import pallas as pl
from jax.experimental.pallas import tpu as pltpu

HEAD_DIM = 128
N_HEADS = 16
N_HEADS_NA = 8
N_HEADS_DIL = N_HEADS - N_HEADS_NA
GRID_W = 64
NA_ROWS = 8
NA_COLS = 16
DIL_PATTERNS = ((128, 1), (512, 4), (2048, 16))
ROPE_THETA = 500000.0
ROPE_DIM = HEAD_DIM // 4
N_MOD = 9
EPS = 1e-6
NEG = -1e30
ATTN_SCALE = HEAD_DIM ** -0.5
LOG2_E = math.log2(math.e)

BF16 = jnp.bfloat16
F32 = jnp.float32

MIB = 1024 * 1024

ADA_TN = 1024
NORM_TM = 512
NORM_ROWS = 32
NORM_UNROLL = 4
FFN_TM = 1024
FFN_TF = 512
FFN_RES = 256
QKV_TM = 512
QKV_TN = 2048
QKV_PIECE = 256
SIDE_CAST_COLS = 512
PREP_ROWS = 512
OPROJ_TM = 512
ATTN_UNROLL = 4
NA_RB = 4
NA_KR = 12
DIL_TQ = 256
DIL_NEAR_PATTERNS = DIL_PATTERNS[:-1]
DIL_FAR = DIL_PATTERNS[-1][1]
DIL_FAR_STEPS = DIL_PATTERNS[-1][0] // 2 // DIL_FAR
DIL_SPAN = max(w // 2 // d * d for w, d in DIL_NEAR_PATTERNS)
DIL_WIN = DIL_TQ + 2 * DIL_SPAN


def _params(sem, vmem_mib):
    return pltpu.CompilerParams(dimension_semantics=sem, vmem_limit_bytes=vmem_mib * MIB)


def _silu(x):
    return x * (1.0 / (1.0 + jnp.exp(-x)))


def _rms(x, gain):
    ms = jnp.mean(x * x, axis=-1, keepdims=True)
    return x * lax.rsqrt(ms + EPS) * gain


def _norm_modulate_rows(x_ref, g_ref, sh_ref, sc_ref, xn_ref):
    n_rows = xn_ref.shape[0]
    gain_scale = g_ref[...] * (1.0 + sc_ref[0])
    shift = sh_ref[0]

    def body(i, carry):
        rows = pl.ds(pl.multiple_of(i * NORM_ROWS, NORM_ROWS), NORM_ROWS)
        xn_ref[rows, :] = (_rms(x_ref[rows, :], gain_scale) + shift).astype(BF16)
        return carry

    lax.fori_loop(0, n_rows // NORM_ROWS, body, 0, unroll=NORM_UNROLL)


def _norm_mod_kernel(x_ref, g_ref, sh_ref, sc_ref, xn_ref):
    _norm_modulate_rows(x_ref.at[0], g_ref, sh_ref, sc_ref, xn_ref.at[0])


def _norm_mod(x, gain, shift, scale):
    b, s, d = x.shape
    tile = pl.BlockSpec((1, NORM_TM, d), lambda bi, i: (bi, i, 0))
    vec = pl.BlockSpec((1, 1, d), lambda bi, i: (bi, 0, 0))
    return pl.pallas_call(
        _norm_mod_kernel,
        grid=(b, s // NORM_TM),
        in_specs=[tile, pl.BlockSpec((1, d), lambda bi, i: (0, 0)), vec, vec],
        out_specs=tile,
        out_shape=jax.ShapeDtypeStruct((b, s, d), BF16),
        compiler_params=_params(("parallel", "parallel"), 24),
        name="norm_mod",
    )(x, gain.reshape(1, d), shift, scale)


def _ada_kernel(c_ref, w_ref, b_ref, o_ref):
    a = _silu(c_ref[...]).astype(BF16)
    w = w_ref[...].astype(BF16)
    o_ref[...] = jnp.dot(a, w, preferred_element_type=F32) + b_ref[...]


def _ada(c_pad, w_ada, b_ada):
    rows, d = c_pad.shape
    n = w_ada.shape[1]
    return pl.pallas_call(
        _ada_kernel,
        grid=(n // ADA_TN,),
        in_specs=[
            pl.BlockSpec((rows, d), lambda j: (0, 0)),
            pl.BlockSpec((d, ADA_TN), lambda j: (0, j)),
            pl.BlockSpec((1, ADA_TN), lambda j: (0, j)),
        ],
        out_specs=pl.BlockSpec((rows, ADA_TN), lambda j: (0, j)),
        out_shape=jax.ShapeDtypeStruct((rows, n), F32),
        compiler_params=_params(("arbitrary",), 24),
        name="ada_mod",
    )(c_pad, w_ada, b_ada.reshape(1, n))


def _ffn_kernel(xn_ref, xres_ref, gt_ref, wg_ref, wu_ref, wd_ref, *rest, emit_next):
    if emit_next:
        gn_ref, shn_ref, scn_ref, o_ref, xnn_ref = rest
    else:
        (o_ref,) = rest
    j = pl.program_id(2)
    n_res = o_ref.shape[2] // FFN_RES

    @pl.when(j == 0)
    def _():
        o_ref[0] = jnp.zeros(o_ref.shape[1:], F32)

    xn = xn_ref[0]
    g = jnp.dot(xn, wg_ref[...], preferred_element_type=F32)
    u = jnp.dot(xn, wu_ref[...], preferred_element_type=F32)
    a = (_silu(g) * u).astype(BF16)
    o_ref[0] += (0.5 * gt_ref[0]) * jnp.dot(a, wd_ref[...], preferred_element_type=F32)

    @pl.when(j < n_res)
    def _():
        cols = pl.ds(pl.multiple_of(j * FFN_RES, FFN_RES), FFN_RES)
        o_ref[0, :, cols] += xres_ref[0]

    if emit_next:
        @pl.when(j == pl.num_programs(2) - 1)
        def _():
            _norm_modulate_rows(o_ref.at[0], gn_ref, shn_ref, scn_ref, xnn_ref.at[0])


def _ffn(xn, x, gate, w_gate, w_up, w_down, next_norm=None):
    b, s, d = x.shape
    dff = w_gate.shape[1]
    n_steps = dff // FFN_TF
    n_res = d // FFN_RES
    assert n_res <= n_steps
    tile = pl.BlockSpec((1, FFN_TM, d), lambda bi, i, j: (bi, i, 0))
    vec = pl.BlockSpec((1, 1, d), lambda bi, i, j: (bi, 0, 0))
    in_specs = [
        tile,
        pl.BlockSpec((1, FFN_TM, FFN_RES), lambda bi, i, j: (bi, i, jnp.minimum(j, n_res - 1))),
        vec,
        pl.BlockSpec((d, FFN_TF), lambda bi, i, j: (0, j)),
        pl.BlockSpec((d, FFN_TF), lambda bi, i, j: (0, j)),
        pl.BlockSpec((FFN_TF, d), lambda bi, i, j: (j, 0)),
    ]
    args = [xn, x, gate, w_gate, w_up, w_down]
    out_specs = [tile]
    out_shape = [jax.ShapeDtypeStruct((b, s, d), F32)]
    if next_norm is not None:
        gain, shift, scale = next_norm
        in_specs += [pl.BlockSpec((1, d), lambda bi, i, j: (0, 0)), vec, vec]
        args += [gain.reshape(1, d), shift, scale]
        out_specs.append(tile)
        out_shape.append(jax.ShapeDtypeStruct((b, s, d), BF16))
    outs = pl.pallas_call(
        functools.partial(_ffn_kernel, emit_next=next_norm is not None),
        grid=(b, s // FFN_TM, n_steps),
        in_specs=in_specs,
        out_specs=out_specs,
        out_shape=out_shape,
        compiler_params=_params(("parallel", "parallel", "arbitrary"), 56),
        name="ffn",
    )(*args)
    return outs if next_norm is not None else outs[0]


def _qkv_kernel(xn_ref, w_ref, hg_ref, cos_ref, sin_ref, o_na_ref, o_dil_ref):
    j = pl.program_id(2)
    n_pieces = QKV_TN // QKV_PIECE
    heads_per_piece = QKV_PIECE // HEAD_DIM

    def store(head, y):
        if head < N_HEADS_NA:
            o_na_ref[head] = y.astype(BF16)
        else:
            o_dil_ref[head - N_HEADS_NA] = y

    def normed(r, head):
        return _rms(r, hg_ref[0, head // N_HEADS_NA])

    def normed_rotary(r, head):
        y = normed(r, head)
        return y * cos_ref[...] + pltpu.roll(y, HEAD_DIM // 2, 1) * sin_ref[...]

    def pieces(epilogue_of_head):
        xn = xn_ref[0]
        for p in list(range(n_pieces // 2, n_pieces)) + list(range(n_pieces // 2)):
            r = jnp.dot(xn, w_ref[:, p * QKV_PIECE:(p + 1) * QKV_PIECE],
                        preferred_element_type=F32)
            for hh in range(heads_per_piece):
                head = p * heads_per_piece + hh
                store(head, epilogue_of_head(head)(r[:, hh * HEAD_DIM:(hh + 1) * HEAD_DIM], head))

    @pl.when(j < 2)
    def _():
        pieces(lambda head: normed if head < N_HEADS_NA else normed_rotary)

    @pl.when(j == 2)
    def _():
        pieces(lambda head: (lambda r, _: r))


def _qkv(xn, w_qkv, head_gains, rope_cos, rope_sin):
    b, s, d = xn.shape
    n = w_qkv.shape[1]
    assert QKV_TN == N_HEADS * HEAD_DIM and n == 3 * QKV_TN
    rope = pl.BlockSpec((QKV_TM, HEAD_DIM), lambda bi, i, j: (i, 0))
    return pl.pallas_call(
        _qkv_kernel,
        grid=(b, s // QKV_TM, n // QKV_TN),
        in_specs=[
            pl.BlockSpec((1, QKV_TM, d), lambda bi, i, j: (bi, i, 0)),
            pl.BlockSpec((d, QKV_TN), lambda bi, i, j: (0, j)),
            pl.BlockSpec((1, 2, 1, HEAD_DIM), lambda bi, i, j: (jnp.minimum(j, 1), 0, 0, 0)),
            rope, rope,
        ],
        out_specs=[
            pl.BlockSpec((None, N_HEADS_NA, QKV_TM, HEAD_DIM), lambda bi, i, j: (bi, j, i, 0)),
            pl.BlockSpec((None, N_HEADS_DIL, QKV_TM, HEAD_DIM), lambda bi, i, j: (bi, j, i, 0)),
        ],
        out_shape=[
            jax.ShapeDtypeStruct((b, 3 * N_HEADS_NA, s, HEAD_DIM), BF16),
            jax.ShapeDtypeStruct((b, 3 * N_HEADS_DIL, s, HEAD_DIM), F32),
        ],
        compiler_params=_params(("parallel", "parallel", "arbitrary"), 48),
        name="qkv",
    )(xn, w_qkv, head_gains, rope_cos, rope_sin)


def _rotary_lane_order():
    half = ROPE_DIM // 2
    mid = HEAD_DIM // 2
    order = (list(range(half)) + list(range(ROPE_DIM, ROPE_DIM + mid - half))
             + list(range(half, ROPE_DIM)) + list(range(ROPE_DIM + mid - half, HEAD_DIM)))
    assert sorted(order) == list(range(HEAD_DIM))
    return order


def _take_lanes(a, order):
    runs, start = [], 0
    for i in range(1, len(order) + 1):
        if i == len(order) or order[i] != order[i - 1] + 1:
            runs.append(a[..., order[start]:order[i - 1] + 1])
            start = i
    return jnp.concatenate(runs, axis=-1)


def _rope_tables(s):
    pos = jnp.arange(s, dtype=F32)
    inv = jnp.power(ROPE_THETA, -jnp.arange(0, ROPE_DIM, 2, dtype=F32) / ROPE_DIM)
    ang = pos[:, None] * inv[None, :]
    cos, sin = jnp.cos(ang), jnp.sin(ang)
    half = ROPE_DIM // 2
    gap = HEAD_DIM // 2 - half
    ones, zeros = jnp.ones((s, gap), F32), jnp.zeros((s, gap), F32)
    rope_cos = jnp.concatenate([cos, ones, cos, ones], axis=1)
    rope_sin = jnp.concatenate([-sin, zeros, sin, zeros], axis=1)
    return rope_cos, rope_sin


def _lane_runs(order):
    runs, start = [], 0
    for i in range(1, len(order) + 1):
        if i == len(order) or order[i] != order[i - 1] + 1:
            runs.append((start, order[start], i - start))
            start = i
    return runs


def _prep_qkv_kernel(w_ref, o_ref):
    j = pl.program_id(1)

    @pl.when(j == 2)
    def _():
        o_ref[...] = w_ref[...].astype(BF16)

    @pl.when(j < 2)
    def _():
        lane = lax.broadcasted_iota(jnp.int32, (w_ref.shape[0], HEAD_DIM), 1)
        runs = _lane_runs(_rotary_lane_order())
        for head in range(N_HEADS):
            cols = slice(head * HEAD_DIM, (head + 1) * HEAD_DIM)
            x = w_ref[:, cols]
            if head >= N_HEADS_NA:
                y = x
                for dst, src, length in runs:
                    if dst != src:
                        moved = pltpu.roll(x, (dst - src) % HEAD_DIM, 1)
                        y = jnp.where(jnp.logical_and(lane >= dst, lane < dst + length), moved, y)
                x = y
            o_ref[:, cols] = x.astype(BF16)


def _prep_qkv_weight(w_qkv):
    d, n = w_qkv.shape
    assert n == 3 * QKV_TN
    block = pl.BlockSpec((PREP_ROWS, QKV_TN), lambda i, j: (i, j))
    return pl.pallas_call(
        _prep_qkv_kernel,
        grid=(d // PREP_ROWS, 3),
        in_specs=[block],
        out_specs=block,
        out_shape=jax.ShapeDtypeStruct((d, n), BF16),
        compiler_params=_params(("parallel", "arbitrary"), 24),
        name="prep_qkv_weight",
    )(w_qkv)


def _na_window_start(ib, rows):
    return jnp.clip(NA_RB * ib - NA_ROWS // 2, 0, rows - NA_KR)


def _na_bias_kernel(rpb_ref, o_ref, *, rows):
    h = pl.program_id(0)
    n_ro = 2 * NA_ROWS - 1
    n_co = 2 * NA_COLS - 1
    c = lax.broadcasted_iota(jnp.int32, (GRID_W, 2 * GRID_W), 0)
    lane = lax.broadcasted_iota(jnp.int32, (GRID_W, 2 * GRID_W), 1)
    kc = lane & (GRID_W - 1)
    diff = kc - c + (NA_COLS - 1)
    cs = jnp.clip(c - NA_COLS // 2, 0, GRID_W - NA_COLS)
    col_ok = jnp.logical_and(kc >= cs, kc < cs + NA_COLS)
    neg = jnp.full((GRID_W, 2 * GRID_W), NEG, F32)
    toeplitz = []
    for ro in range(n_ro):
        t = jnp.zeros((GRID_W, 2 * GRID_W), F32)
        for m in range(n_co):
            t = jnp.where(diff == m, rpb_ref[(h * n_ro + ro) * n_co + m], t)
        toeplitz.append(jnp.where(col_ok, t * LOG2_E, neg))
    low_half = lane < GRID_W
    n_blocks = rows // NA_RB
    for var, ib in enumerate((0, n_blocks // 2, n_blocks - 1)):
        ws = min(max(NA_RB * ib - NA_ROWS // 2, 0), rows - NA_KR)
        for qr in range(NA_RB):
            r = NA_RB * ib + qr
            r_start = min(max(r - NA_ROWS // 2, 0), rows - NA_ROWS)
            for g in range(NA_KR // 2):
                pieces = []
                for kr in (2 * g, 2 * g + 1):
                    ka = ws + kr
                    inside = r_start <= ka < r_start + NA_ROWS
                    pieces.append(toeplitz[ka - r + NA_ROWS - 1] if inside else neg)
                o_ref[0, var, qr * GRID_W:(qr + 1) * GRID_W, g * 2 * GRID_W:(g + 1) * 2 * GRID_W] = (
                    jnp.where(low_half, pieces[0], pieces[1]))


def _na_bias(rpb, rows):
    nh = rpb.shape[0]
    return pl.pallas_call(
        functools.partial(_na_bias_kernel, rows=rows),
        grid=(nh,),
        in_specs=[pl.BlockSpec(memory_space=pltpu.SMEM)],
        out_specs=pl.BlockSpec((1, 3, NA_RB * GRID_W, NA_KR * GRID_W), lambda h: (h, 0, 0, 0)),
        out_shape=jax.ShapeDtypeStruct((nh, 3, NA_RB * GRID_W, NA_KR * GRID_W), F32),
        compiler_params=_params(("arbitrary",), 24),
        name="na_bias",
    )(rpb.reshape(-1))


def _softmax_pv(s2, v):
    m = jnp.max(s2, axis=-1, keepdims=True)
    p = jnp.exp2(s2 - m)
    l = jnp.sum(p, axis=-1, keepdims=True)
    return jnp.dot(p.astype(BF16), v, preferred_element_type=F32) / l


def _side_cast(refs):
    n = len(refs) // 2
    for src_ref, dst_ref in zip(refs[:n], refs[n:]):
        for c0 in range(0, src_ref.shape[1], SIDE_CAST_COLS):
            cols = slice(c0, c0 + SIDE_CAST_COLS)
            dst_ref[:, cols] = src_ref[:, cols].astype(BF16)


def _side_cast_plumbing(weights, n_steps, step_of):
    specs, shapes = [], []
    for w in weights:
        slab = w.shape[0] // n_steps
        assert slab * n_steps == w.shape[0] and w.shape[1] % SIDE_CAST_COLS == 0
        specs.append(pl.BlockSpec((slab, w.shape[1]), lambda *g: (step_of(*g), 0)))
        shapes.append(jax.ShapeDtypeStruct(w.shape, BF16))
    return specs, shapes


def _na_kernel(q_ref, k_ref, v_ref, bias_ref, *rest, rows):
    n_side = (len(rest) - 1) // 2
    o_ref = rest[n_side]
    _side_cast(rest[:n_side] + rest[n_side + 1:])
    n_blocks = rows // NA_RB
    tq = NA_RB * GRID_W

    def block(ib):
        start = pl.multiple_of(_na_window_start(ib, rows) * GRID_W, GRID_W)
        variant = jnp.where(ib == 0, 0, jnp.where(ib == n_blocks - 1, 2, 1))
        qrows = pl.ds(pl.multiple_of(ib * tq, tq), tq)
        k = k_ref[0, pl.ds(start, NA_KR * GRID_W), :]
        v = v_ref[0, pl.ds(start, NA_KR * GRID_W), :]
        s = lax.dot_general(q_ref[0, qrows, :], k, (((1,), (1,)), ((), ())),
                            preferred_element_type=F32)
        o_ref[0, qrows, :] = _softmax_pv(s + bias_ref[0, variant], v)

    def body(i, carry):
        for u in range(ATTN_UNROLL):
            block(i * ATTN_UNROLL + u)
        return carry

    lax.fori_loop(0, n_blocks // ATTN_UNROLL, body, 0)


def _na_attention(qkv, bias, side_weights=()):
    b, _, s, _ = qkv.shape
    rows = s // GRID_W
    heads_total = N_HEADS_NA
    tq = NA_RB * GRID_W
    side_specs, side_shapes = _side_cast_plumbing(
        side_weights, b * N_HEADS_NA, lambda bi, h: bi * N_HEADS_NA + h)
    head_slab = lambda first: pl.BlockSpec((None, 1, s, HEAD_DIM), lambda bi, h: (bi, first + h, 0, 0))
    return pl.pallas_call(
        functools.partial(_na_kernel, rows=rows),
        grid=(b, N_HEADS_NA),
        in_specs=[
            head_slab(0), head_slab(heads_total), head_slab(2 * heads_total),
            pl.BlockSpec((1, 3, tq, NA_KR * GRID_W), lambda bi, h: (h, 0, 0, 0)),
        ] + side_specs,
        out_specs=[pl.BlockSpec((1, s, HEAD_DIM), lambda bi, h: (bi, 0, h))] + side_specs,
        out_shape=[jax.ShapeDtypeStruct((b, s, N_HEADS_NA * HEAD_DIM), F32)] + side_shapes,
        compiler_params=_params(("parallel", "arbitrary"), 52),
        name="na_attn",
    )(qkv, qkv, qkv, bias, *side_weights)


def _dil_near_table():
    q = np.arange(DIL_TQ)[:, None]
    x = np.arange(DIL_WIN + 2 * DIL_SPAN)[None, :]
    d = x - 2 * DIL_SPAN - q
    mult = np.zeros(d.shape, np.int64)
    for window, dil in DIL_NEAR_PATTERNS:
        reach = (window // 2 // dil) * dil
        mult += ((d % dil == 0) & (np.abs(d) <= reach)).astype(np.int64)
    table = np.where(mult > 0, np.log2(np.maximum(mult, 1)), NEG)
    return jnp.asarray(table, F32)


def _dil_far_band(length):
    i = np.arange(length)
    inside = np.abs(i[:, None] - i[None, :]) <= DIL_FAR_STEPS
    return jnp.asarray(np.where(inside, 0.0, NEG), F32)


def _dil_kernel(q_ref, k_ref, v_ref, t_ref, band_ref, *rest, seq):
    ofar_ref, lsefar_ref = rest[-2:]
    rest = rest[:-2]
    n_side = (len(rest) - 1) // 2
    o_ref = rest[n_side]
    _side_cast(rest[:n_side] + rest[n_side + 1:])
    length = seq // DIL_FAR
    contract_last = (((1,), (1,)), ((), ()))

    def far_block(r):
        rows = pl.ds(r, length, stride=DIL_FAR)
        q, k, v = (ref[0, rows, :].astype(BF16) for ref in (q_ref, k_ref, v_ref))
        s = lax.dot_general(q, k, contract_last, preferred_element_type=F32) + band_ref[...]
        m = jnp.max(s, axis=-1, keepdims=True)
        p = jnp.exp2(s - m)
        l = jnp.sum(p, axis=-1, keepdims=True)
        o = jnp.dot(p.astype(BF16), v, preferred_element_type=F32) / l
        ofar_ref[rows, :] = o
        lsefar_ref[rows, :] = jnp.broadcast_to(m + jnp.log2(l), o.shape)

    def near_block(ib):
        t0 = ib * DIL_TQ
        ws = jnp.clip(t0 - DIL_SPAN, 0, seq - DIL_WIN)
        off = pl.multiple_of(ws - t0 + 2 * DIL_SPAN, DIL_TQ)
        ws = pl.multiple_of(ws, DIL_TQ)
        qrows = pl.ds(pl.multiple_of(t0, DIL_TQ), DIL_TQ)
        k = k_ref[0, pl.ds(ws, DIL_WIN), :].astype(BF16)
        v = v_ref[0, pl.ds(ws, DIL_WIN), :].astype(BF16)
        s = lax.dot_general(q_ref[0, qrows, :].astype(BF16), k, contract_last,
                            preferred_element_type=F32)
        s = s + t_ref[:, pl.ds(off, DIL_WIN)]
        m = jnp.max(s, axis=-1, keepdims=True)
        p = jnp.exp2(s - m)
        l = jnp.sum(p, axis=-1, keepdims=True)
        acc = jnp.dot(p.astype(BF16), v, preferred_element_type=F32)
        lse_far = lsefar_ref[qrows, :]
        top = jnp.maximum(m, lse_far)
        w_near = jnp.exp2(m - top)
        w_far = jnp.exp2(lse_far - top)
        o_ref[0, qrows, :] = (w_near * acc + w_far * ofar_ref[qrows, :]) / (w_near * l + w_far)

    def unrolled(block):
        def body(i, carry):
            for u in range(ATTN_UNROLL):
                block(i * ATTN_UNROLL + u)
            return carry
        return body

    lax.fori_loop(0, DIL_FAR // ATTN_UNROLL, unrolled(far_block), 0)
    lax.fori_loop(0, seq // DIL_TQ // ATTN_UNROLL, unrolled(near_block), 0)


def _dil_attention(qkv, table, band, side_weights=()):
    b, _, s, _ = qkv.shape
    length = s // DIL_FAR
    assert band.shape == (length, length) and length == DIL_TQ
    side_specs, side_shapes = _side_cast_plumbing(
        side_weights, b * N_HEADS_DIL, lambda bi, h: bi * N_HEADS_DIL + h)
    head_slab = lambda first: pl.BlockSpec((None, 1, s, HEAD_DIM), lambda bi, h: (bi, first + h, 0, 0))
    return pl.pallas_call(
        functools.partial(_dil_kernel, seq=s),
        grid=(b, N_HEADS_DIL),
        in_specs=[
            head_slab(0), head_slab(N_HEADS_DIL), head_slab(2 * N_HEADS_DIL),
            pl.BlockSpec(table.shape, lambda bi, h: (0, 0)),
            pl.BlockSpec(band.shape, lambda bi, h: (0, 0)),
        ] + side_specs,
        out_specs=[pl.BlockSpec((1, s, HEAD_DIM), lambda bi, h: (bi, 0, h))] + side_specs,
        out_shape=[jax.ShapeDtypeStruct((b, s, N_HEADS_DIL * HEAD_DIM), F32)] + side_shapes,
        scratch_shapes=[pltpu.VMEM((s, HEAD_DIM), F32), pltpu.VMEM((s, HEAD_DIM), F32)],
        compiler_params=_params(("parallel", "arbitrary"), 52),
        name="dil_attn",
    )(qkv, qkv, qkv, table, band, *side_weights)


def _oproj_kernel(ona_ref, odil_ref, h_ref, gna_ref, gdil_ref, gt_ref, w_ref,
                  gn_ref, shn_ref, scn_ref, o_ref, xnn_ref):
    d_na = ona_ref.shape[2]
    na = _rms(ona_ref[0], gna_ref[...]).astype(BF16)
    nd = _rms(odil_ref[0], gdil_ref[...]).astype(BF16)
    mix = jnp.dot(na, w_ref[:d_na, :], preferred_element_type=F32)
    mix = mix + jnp.dot(nd, w_ref[d_na:, :], preferred_element_type=F32)
    o_ref[0] = h_ref[0] + gt_ref[0] * mix
    _norm_modulate_rows(o_ref.at[0], gn_ref, shn_ref, scn_ref, xnn_ref.at[0])


def _oproj(o_na, o_dil, h, g_na, g_dil, gate, w_o, next_norm):
    b, s, d = h.shape
    d_na, d_dil = o_na.shape[2], o_dil.shape[2]
    gain, shift, scale = next_norm
    tile = pl.BlockSpec((1, OPROJ_TM, d), lambda bi, i: (bi, i, 0))
    vec = pl.BlockSpec((1, 1, d), lambda bi, i: (bi, 0, 0))
    return pl.pallas_call(
        _oproj_kernel,
        grid=(b, s // OPROJ_TM),
        in_specs=[
            pl.BlockSpec((1, OPROJ_TM, d_na), lambda bi, i: (bi, i, 0)),
            pl.BlockSpec((1, OPROJ_TM, d_dil), lambda bi, i: (bi, i, 0)),
            tile,
            pl.BlockSpec((1, d_na), lambda bi, i: (0, 0)),
            pl.BlockSpec((1, d_dil), lambda bi, i: (0, 0)),
            vec,
            pl.BlockSpec((d, d), lambda bi, i: (0, 0)),
            pl.BlockSpec((1, d), lambda bi, i: (0, 0)), vec, vec,
        ],
        out_specs=[tile, tile],
        out_shape=[jax.ShapeDtypeStruct((b, s, d), F32), jax.ShapeDtypeStruct((b, s, d), BF16)],
        compiler_params=_params(("parallel", "parallel"), 52),
        name="oproj",
    )(o_na, o_dil, h, g_na.reshape(1, d_na), g_dil.reshape(1, d_dil), gate, w_o,
      gain.reshape(1, d), shift, scale)


def kernel(x, c, w_ada, b_ada, g_ffn1, w1_gate, w1_up, w1_down, g_mix, w_qkv, qn_na, kn_na, qn_dil, kn_dil, rpb_na, g_out_na, g_out_dil, w_o, g_ffn2, w2_gate, w2_up, w2_down):
    b, s, d = x.shape
    depth = w_ada.shape[0]
    rows = s // GRID_W
    rope_cos, rope_sin = _rope_tables(s)
    dil_table = _dil_near_table()
    dil_band = _dil_far_band(s // DIL_FAR)
    c_pad = jnp.pad(c, ((0, 8 - b), (0, 0)))
    lane_order = _rotary_lane_order()
    q_factor = ATTN_SCALE * LOG2_E

    h = x
    for l in range(depth):
        mod = _ada(c_pad, w_ada[l], b_ada[l])[:b]
        sh1, sc1, gt1, sh2, sc2, gt2, sh3, sc3, gt3 = [
            m.reshape(b, 1, d) for m in jnp.split(mod, N_MOD, axis=-1)]

        xn = _norm_mod(h, g_ffn1[l], sh1, sc1)
        h, xn = _ffn(xn, h, gt1,
                     w1_gate[l].astype(BF16), w1_up[l].astype(BF16), w1_down[l].astype(BF16),
                     next_norm=(g_mix[l], sh2, sc2))

        head_gains = jnp.stack([
            jnp.stack([qn_na[l] * q_factor, _take_lanes(qn_dil[l], lane_order) * q_factor]),
            jnp.stack([kn_na[l], _take_lanes(kn_dil[l], lane_order)]),
        ]).reshape(2, 2, 1, HEAD_DIM)
        qkv_na, qkv_dil = _qkv(xn, _prep_qkv_weight(w_qkv[l]), head_gains, rope_cos, rope_sin)

        o_na, w2g, w2u, wo = _na_attention(qkv_na, _na_bias(rpb_na[l], rows),
                                           side_weights=(w2_gate[l], w2_up[l], w_o[l]))
        o_dil, w2d = _dil_attention(qkv_dil, dil_table, dil_band, side_weights=(w2_down[l],))
        h, xn = _oproj(o_na, o_dil, h, g_out_na[l], g_out_dil[l], gt2, wo,
                       next_norm=(g_ffn2[l], sh3, sc3))

        h = _ffn(xn, h, gt3, w2g, w2u, w2d)
    return h
```

```python
import functools
import math

import numpy as np
import jax
import jax.numpy as jnp
from jax import lax
from jax.experimental import pallas as pl
from jax.experimental.pallas import tpu as pltpu

HEAD_DIM = 128
N_HEADS = 16
N_HEADS_NA = 8
N_HEADS_DIL = N_HEADS - N_HEADS_NA
GRID_W = 64
NA_ROWS = 8
NA_COLS = 16
DIL_PATTERNS = ((128, 1), (512, 4), (2048, 16))
ROPE_THETA = 500000.0
ROPE_DIM = HEAD_DIM // 4
N_MOD = 9
EPS = 1e-6
NEG = -1e30
ATTN_SCALE = HEAD_DIM ** -0.5
LOG2_E = math.log2(math.e)

BF16 = jnp.bfloat16
F32 = jnp.float32

MIB = 1024 * 1024
HBM_OUTPUT_VMEM_MIB = 56

ADA_TN = 1024
NORM_TM = 512
NORM_ROWS = 32
NORM_UNROLL = 4
FFN_TM = 1024
FFN_TF = 512
FFN_RES = 256
QKV_TM = 512
QKV_TN = 2048
QKV_PIECE = 256
SIDE_CAST_COLS = 512
PREP_ROWS = 512
OPROJ_TM = 512
ATTN_UNROLL = 4
NA_RB = 4
NA_KR = 12
DIL_TQ = 256
DIL_NEAR_PATTERNS = DIL_PATTERNS[:-1]
DIL_FAR = DIL_PATTERNS[-1][1]
DIL_FAR_STEPS = DIL_PATTERNS[-1][0] // 2 // DIL_FAR
DIL_SPAN = max(w // 2 // d * d for w, d in DIL_NEAR_PATTERNS)
DIL_WIN = DIL_TQ + 2 * DIL_SPAN


def _params(sem, vmem_mib):
    return pltpu.CompilerParams(dimension_semantics=sem, vmem_limit_bytes=vmem_mib * MIB)


def _silu(x):
    return x * (1.0 / (1.0 + jnp.exp(-x)))


def _rms(x, gain):
    ms = jnp.mean(x * x, axis=-1, keepdims=True)
    return x * lax.rsqrt(ms + EPS) * gain


def _norm_modulate_rows(x_ref, g_ref, sh_ref, sc_ref, xn_ref):
    n_rows = xn_ref.shape[0]
    gain_scale = g_ref[...] * (1.0 + sc_ref[0])
    shift = sh_ref[0]

    def body(i, carry):
        rows = pl.ds(pl.multiple_of(i * NORM_ROWS, NORM_ROWS), NORM_ROWS)
        xn_ref[rows, :] = (_rms(x_ref[rows, :], gain_scale) + shift).astype(BF16)
        return carry

    lax.fori_loop(0, n_rows // NORM_ROWS, body, 0, unroll=NORM_UNROLL)


def _norm_mod_kernel(x_ref, g_ref, sh_ref, sc_ref, xn_ref):
    _norm_modulate_rows(x_ref.at[0], g_ref, sh_ref, sc_ref, xn_ref.at[0])


def _norm_mod(x, gain, shift, scale):
    b, s, d = x.shape
    tile = pl.BlockSpec((1, NORM_TM, d), lambda bi, i: (bi, i, 0))
    vec = pl.BlockSpec((1, 1, d), lambda bi, i: (bi, 0, 0))
    return pl.pallas_call(
        _norm_mod_kernel,
        grid=(b, s // NORM_TM),
        in_specs=[tile, pl.BlockSpec((1, d), lambda bi, i: (0, 0)), vec, vec],
        out_specs=tile,
        out_shape=jax.ShapeDtypeStruct((b, s, d), BF16),
        compiler_params=_params(("parallel", "parallel"), 24),
        name="norm_mod",
    )(x, gain.reshape(1, d), shift, scale)


def _ada_kernel(c_ref, w_ref, b_ref, o_ref):
    a = _silu(c_ref[...]).astype(BF16)
    w = w_ref[...].astype(BF16)
    o_ref[...] = jnp.dot(a, w, preferred_element_type=F32) + b_ref[...]


def _ada(c_pad, w_ada, b_ada):
    rows, d = c_pad.shape
    n = w_ada.shape[1]
    return pl.pallas_call(
        _ada_kernel,
        grid=(n // ADA_TN,),
        in_specs=[
            pl.BlockSpec((rows, d), lambda j: (0, 0)),
            pl.BlockSpec((d, ADA_TN), lambda j: (0, j)),
            pl.BlockSpec((1, ADA_TN), lambda j: (0, j)),
        ],
        out_specs=pl.BlockSpec((rows, ADA_TN), lambda j: (0, j)),
        out_shape=jax.ShapeDtypeStruct((rows, n), F32),
        compiler_params=_params(("arbitrary",), 24),
        name="ada_mod",
    )(c_pad, w_ada, b_ada.reshape(1, n))


def _ffn_kernel(xn_ref, xres_ref, gt_ref, wg_ref, wu_ref, wd_ref, *rest, emit_next):
    if emit_next:
        gn_ref, shn_ref, scn_ref, o_ref, xnn_ref = rest
    else:
        (o_ref,) = rest
    j = pl.program_id(2)
    n_res = o_ref.shape[2] // FFN_RES

    @pl.when(j == 0)
    def _():
        o_ref[0] = jnp.zeros(o_ref.shape[1:], F32)

    xn = xn_ref[0]
    g = jnp.dot(xn, wg_ref[...], preferred_element_type=F32)
    u = jnp.dot(xn, wu_ref[...], preferred_element_type=F32)
    a = (_silu(g) * u).astype(BF16)
    o_ref[0] += (0.5 * gt_ref[0]) * jnp.dot(a, wd_ref[...], preferred_element_type=F32)

    @pl.when(j < n_res)
    def _():
        cols = pl.ds(pl.multiple_of(j * FFN_RES, FFN_RES), FFN_RES)
        o_ref[0, :, cols] += xres_ref[0]

    if emit_next:
        @pl.when(j == pl.num_programs(2) - 1)
        def _():
            _norm_modulate_rows(o_ref.at[0], gn_ref, shn_ref, scn_ref, xnn_ref.at[0])


def _ffn(xn, x, gate, w_gate, w_up, w_down, next_norm=None):
    b, s, d = x.shape
    dff = w_gate.shape[1]
    n_steps = dff // FFN_TF
    n_res = d // FFN_RES
    assert n_res <= n_steps
    tile = pl.BlockSpec((1, FFN_TM, d), lambda bi, i, j: (bi, i, 0))
    vec = pl.BlockSpec((1, 1, d), lambda bi, i, j: (bi, 0, 0))
    in_specs = [
        tile,
        pl.BlockSpec((1, FFN_TM, FFN_RES), lambda bi, i, j: (bi, i, jnp.minimum(j, n_res - 1))),
        vec,
        pl.BlockSpec((d, FFN_TF), lambda bi, i, j: (0, j)),
        pl.BlockSpec((d, FFN_TF), lambda bi, i, j: (0, j)),
        pl.BlockSpec((FFN_TF, d), lambda bi, i, j: (j, 0)),
    ]
    args = [xn, x, gate, w_gate, w_up, w_down]
    out_specs = [tile]
    out_shape = [jax.ShapeDtypeStruct((b, s, d), F32)]
    if next_norm is not None:
        gain, shift, scale = next_norm
        in_specs += [pl.BlockSpec((1, d), lambda bi, i, j: (0, 0)), vec, vec]
        args += [gain.reshape(1, d), shift, scale]
        out_specs.append(tile)
        out_shape.append(jax.ShapeDtypeStruct((b, s, d), BF16))
    outs = pl.pallas_call(
        functools.partial(_ffn_kernel, emit_next=next_norm is not None),
        grid=(b, s // FFN_TM, n_steps),
        in_specs=in_specs,
        out_specs=out_specs,
        out_shape=out_shape,
        compiler_params=_params(("parallel", "parallel", "arbitrary"), 56),
        name="ffn",
    )(*args)
    return outs if next_norm is not None else outs[0]


def _qkv_kernel(xn_ref, w_ref, hg_ref, cos_ref, sin_ref, o_na_ref, o_dil_ref):
    j = pl.program_id(2)
    n_pieces = QKV_TN // QKV_PIECE
    heads_per_piece = QKV_PIECE // HEAD_DIM

    def store(head, y):
        if head < N_HEADS_NA:
            o_na_ref[head] = y.astype(BF16)
        else:
            o_dil_ref[head - N_HEADS_NA] = y

    def normed(r, head):
        return _rms(r, hg_ref[0, head // N_HEADS_NA])

    def normed_rotary(r, head):
        y = normed(r, head)
        return y * cos_ref[...] + pltpu.roll(y, HEAD_DIM // 2, 1) * sin_ref[...]

    def pieces(epilogue_of_head):
        xn = xn_ref[0]
        for p in list(range(n_pieces // 2, n_pieces)) + list(range(n_pieces // 2)):
            r = jnp.dot(xn, w_ref[:, p * QKV_PIECE:(p + 1) * QKV_PIECE],
                        preferred_element_type=F32)
            for hh in range(heads_per_piece):
                head = p * heads_per_piece + hh
                store(head, epilogue_of_head(head)(r[:, hh * HEAD_DIM:(hh + 1) * HEAD_DIM], head))

    @pl.when(j < 2)
    def _():
        pieces(lambda head: normed if head < N_HEADS_NA else normed_rotary)

    @pl.when(j == 2)
    def _():
        pieces(lambda head: (lambda r, _: r))


def _qkv(xn, w_qkv, head_gains, rope_cos, rope_sin):
    b, s, d = xn.shape
    n = w_qkv.shape[1]
    assert QKV_TN == N_HEADS * HEAD_DIM and n == 3 * QKV_TN
    rope = pl.BlockSpec((QKV_TM, HEAD_DIM), lambda bi, i, j: (i, 0))
    return pl.pallas_call(
        _qkv_kernel,
        grid=(b, s // QKV_TM, n // QKV_TN),
        in_specs=[
            pl.BlockSpec((1, QKV_TM, d), lambda bi, i, j: (bi, i, 0)),
            pl.BlockSpec((d, QKV_TN), lambda bi, i, j: (0, j)),
            pl.BlockSpec((1, 2, 1, HEAD_DIM), lambda bi, i, j: (jnp.minimum(j, 1), 0, 0, 0)),
            rope, rope,
        ],
        out_specs=[
            pl.BlockSpec((None, N_HEADS_NA, QKV_TM, HEAD_DIM), lambda bi, i, j: (bi, j, i, 0)),
            pl.BlockSpec((None, N_HEADS_DIL, QKV_TM, HEAD_DIM), lambda bi, i, j: (bi, j, i, 0)),
        ],
        out_shape=[
            jax.ShapeDtypeStruct((b, 3 * N_HEADS_NA, s, HEAD_DIM), BF16),
            jax.ShapeDtypeStruct((b, 3 * N_HEADS_DIL, s, HEAD_DIM), F32),
        ],
        compiler_params=_params(("parallel", "parallel", "arbitrary"), 48),
        name="qkv",
    )(xn, w_qkv, head_gains, rope_cos, rope_sin)


def _rotary_lane_order():
    half = ROPE_DIM // 2
    mid = HEAD_DIM // 2
    order = (list(range(half)) + list(range(ROPE_DIM, ROPE_DIM + mid - half))
             + list(range(half, ROPE_DIM)) + list(range(ROPE_DIM + mid - half, HEAD_DIM)))
    assert sorted(order) == list(range(HEAD_DIM))
    return order


def _take_lanes(a, order):
    runs, start = [], 0
    for i in range(1, len(order) + 1):
        if i == len(order) or order[i] != order[i - 1] + 1:
            runs.append(a[..., order[start]:order[i - 1] + 1])
            start = i
    return jnp.concatenate(runs, axis=-1)


def _rope_tables(s):
    f32 = np.float32
    pos = np.arange(s, dtype=f32)
    inv = np.power(f32(ROPE_THETA), -np.arange(0, ROPE_DIM, 2, dtype=f32) / f32(ROPE_DIM)).astype(f32)
    ang = (pos[:, None] * inv[None, :]).astype(f32)
    cos, sin = np.cos(ang).astype(f32), np.sin(ang).astype(f32)
    half = ROPE_DIM // 2
    gap = HEAD_DIM // 2 - half
    ones, zeros = np.ones((s, gap), f32), np.zeros((s, gap), f32)
    rope_cos = np.concatenate([cos, ones, cos, ones], axis=1)
    rope_sin = np.concatenate([-sin, zeros, sin, zeros], axis=1)
    return jnp.asarray(rope_cos), jnp.asarray(rope_sin)


def _lane_runs(order):
    runs, start = [], 0
    for i in range(1, len(order) + 1):
        if i == len(order) or order[i] != order[i - 1] + 1:
            runs.append((start, order[start], i - start))
            start = i
    return runs


def _prep_qkv_kernel(w_ref, o_ref):
    j = pl.program_id(1)

    @pl.when(j == 2)
    def _():
        o_ref[...] = w_ref[...].astype(BF16)

    @pl.when(j < 2)
    def _():
        lane = lax.broadcasted_iota(jnp.int32, (w_ref.shape[0], HEAD_DIM), 1)
        runs = _lane_runs(_rotary_lane_order())
        for head in range(N_HEADS):
            cols = slice(head * HEAD_DIM, (head + 1) * HEAD_DIM)
            x = w_ref[:, cols]
            if head >= N_HEADS_NA:
                y = x
                for dst, src, length in runs:
                    if dst != src:
                        moved = pltpu.roll(x, (dst - src) % HEAD_DIM, 1)
                        y = jnp.where(jnp.logical_and(lane >= dst, lane < dst + length), moved, y)
                x = y
            o_ref[:, cols] = x.astype(BF16)


def _prep_qkv_weight(w_qkv):
    d, n = w_qkv.shape
    assert n == 3 * QKV_TN
    block = pl.BlockSpec((PREP_ROWS, QKV_TN), lambda i, j: (i, j))
    return pl.pallas_call(
        _prep_qkv_kernel,
        grid=(d // PREP_ROWS, 3),
        in_specs=[block],
        out_specs=block,
        out_shape=jax.ShapeDtypeStruct((d, n), BF16),
        compiler_params=_params(("parallel", "arbitrary"), HBM_OUTPUT_VMEM_MIB),
        name="prep_qkv_weight",
    )(w_qkv)


def _na_window_start(ib, rows):
    return jnp.clip(NA_RB * ib - NA_ROWS // 2, 0, rows - NA_KR)


def _na_bias_kernel(rpb_ref, o_ref, *, rows):
    h = pl.program_id(0)
    n_ro = 2 * NA_ROWS - 1
    n_co = 2 * NA_COLS - 1
    c = lax.broadcasted_iota(jnp.int32, (GRID_W, 2 * GRID_W), 0)
    lane = lax.broadcasted_iota(jnp.int32, (GRID_W, 2 * GRID_W), 1)
    kc = lane & (GRID_W - 1)
    diff = kc - c + (NA_COLS - 1)
    cs = jnp.clip(c - NA_COLS // 2, 0, GRID_W - NA_COLS)
    col_ok = jnp.logical_and(kc >= cs, kc < cs + NA_COLS)
    neg = jnp.full((GRID_W, 2 * GRID_W), NEG, F32)
    toeplitz = []
    for ro in range(n_ro):
        t = jnp.zeros((GRID_W, 2 * GRID_W), F32)
        for m in range(n_co):
            t = jnp.where(diff == m, rpb_ref[(h * n_ro + ro) * n_co + m], t)
        toeplitz.append(jnp.where(col_ok, t * LOG2_E, neg))
    low_half = lane < GRID_W
    n_blocks = rows // NA_RB
    for var, ib in enumerate((0, n_blocks // 2, n_blocks - 1)):
        ws = min(max(NA_RB * ib - NA_ROWS // 2, 0), rows - NA_KR)
        for qr in range(NA_RB):
            r = NA_RB * ib + qr
            r_start = min(max(r - NA_ROWS // 2, 0), rows - NA_ROWS)
            for g in range(NA_KR // 2):
                pieces = []
                for kr in (2 * g, 2 * g + 1):
                    ka = ws + kr
                    inside = r_start <= ka < r_start + NA_ROWS
                    pieces.append(toeplitz[ka - r + NA_ROWS - 1] if inside else neg)
                o_ref[0, var, qr * GRID_W:(qr + 1) * GRID_W, g * 2 * GRID_W:(g + 1) * 2 * GRID_W] = (
                    jnp.where(low_half, pieces[0], pieces[1]))


def _na_bias(rpb, rows):
    nh = rpb.shape[0]
    return pl.pallas_call(
        functools.partial(_na_bias_kernel, rows=rows),
        grid=(nh,),
        in_specs=[pl.BlockSpec(memory_space=pltpu.SMEM)],
        out_specs=pl.BlockSpec((1, 3, NA_RB * GRID_W, NA_KR * GRID_W), lambda h: (h, 0, 0, 0)),
        out_shape=jax.ShapeDtypeStruct((nh, 3, NA_RB * GRID_W, NA_KR * GRID_W), F32),
        compiler_params=_params(("arbitrary",), HBM_OUTPUT_VMEM_MIB),
        name="na_bias",
    )(rpb.reshape(-1))


def _softmax_pv(s2, v):
    m = jnp.max(s2, axis=-1, keepdims=True)
    p = jnp.exp2(s2 - m)
    l = jnp.sum(p, axis=-1, keepdims=True)
    return jnp.dot(p.astype(BF16), v, preferred_element_type=F32) / l


def _side_cast(refs):
    n = len(refs) // 2
    for src_ref, dst_ref in zip(refs[:n], refs[n:]):
        for c0 in range(0, src_ref.shape[1], SIDE_CAST_COLS):
            cols = slice(c0, c0 + SIDE_CAST_COLS)
            dst_ref[:, cols] = src_ref[:, cols].astype(BF16)


def _side_cast_plumbing(weights, n_steps, step_of):
    specs, shapes = [], []
    for w in weights:
        slab = w.shape[0] // n_steps
        assert slab * n_steps == w.shape[0] and w.shape[1] % SIDE_CAST_COLS == 0
        specs.append(pl.BlockSpec((slab, w.shape[1]), lambda *g: (step_of(*g), 0)))
        shapes.append(jax.ShapeDtypeStruct(w.shape, BF16))
    return specs, shapes


def _na_kernel(q_ref, k_ref, v_ref, bias_ref, *rest, rows):
    n_side = (len(rest) - 1) // 2
    o_ref = rest[n_side]
    _side_cast(rest[:n_side] + rest[n_side + 1:])
    n_blocks = rows // NA_RB
    tq = NA_RB * GRID_W

    def block(ib):
        start = pl.multiple_of(_na_window_start(ib, rows) * GRID_W, GRID_W)
        variant = jnp.where(ib == 0, 0, jnp.where(ib == n_blocks - 1, 2, 1))
        qrows = pl.ds(pl.multiple_of(ib * tq, tq), tq)
        k = k_ref[0, pl.ds(start, NA_KR * GRID_W), :]
        v = v_ref[0, pl.ds(start, NA_KR * GRID_W), :]
        s = lax.dot_general(q_ref[0, qrows, :], k, (((1,), (1,)), ((), ())),
                            preferred_element_type=F32)
        o_ref[0, qrows, :] = _softmax_pv(s + bias_ref[0, variant], v)

    def body(i, carry):
        for u in range(ATTN_UNROLL):
            block(i * ATTN_UNROLL + u)
        return carry

    lax.fori_loop(0, n_blocks // ATTN_UNROLL, body, 0)


def _na_attention(qkv, bias, side_weights=()):
    b, _, s, _ = qkv.shape
    rows = s // GRID_W
    heads_total = N_HEADS_NA
    tq = NA_RB * GRID_W
    side_specs, side_shapes = _side_cast_plumbing(
        side_weights, b * N_HEADS_NA, lambda bi, h: bi * N_HEADS_NA + h)
    head_slab = lambda first: pl.BlockSpec((None, 1, s, HEAD_DIM), lambda bi, h: (bi, first + h, 0, 0))
    return pl.pallas_call(
        functools.partial(_na_kernel, rows=rows),
        grid=(b, N_HEADS_NA),
        in_specs=[
            head_slab(0), head_slab(heads_total), head_slab(2 * heads_total),
            pl.BlockSpec((1, 3, tq, NA_KR * GRID_W), lambda bi, h: (h, 0, 0, 0)),
        ] + side_specs,
        out_specs=[pl.BlockSpec((1, s, HEAD_DIM), lambda bi, h: (bi, 0, h))] + side_specs,
        out_shape=[jax.ShapeDtypeStruct((b, s, N_HEADS_NA * HEAD_DIM), F32)] + side_shapes,
        compiler_params=_params(("parallel", "arbitrary"), 52),
        name="na_attn",
    )(qkv, qkv, qkv, bias, *side_weights)


def _dil_near_table():
    q = np.arange(DIL_TQ)[:, None]
    x = np.arange(DIL_WIN + 2 * DIL_SPAN)[None, :]
    d = x - 2 * DIL_SPAN - q
    mult = np.zeros(d.shape, np.int64)
    for window, dil in DIL_NEAR_PATTERNS:
        reach = (window // 2 // dil) * dil
        mult += ((d % dil == 0) & (np.abs(d) <= reach)).astype(np.int64)
    table = np.where(mult > 0, np.log2(np.maximum(mult, 1)), NEG)
    return jnp.asarray(table, F32)


def _dil_far_band(length):
    i = np.arange(length)
    inside = np.abs(i[:, None] - i[None, :]) <= DIL_FAR_STEPS
    return jnp.asarray(np.where(inside, 0.0, NEG), F32)


def _dil_kernel(q_ref, k_ref, v_ref, t_ref, band_ref, *rest, seq):
    ofar_ref, lsefar_ref = rest[-2:]
    rest = rest[:-2]
    n_side = (len(rest) - 1) // 2
    o_ref = rest[n_side]
    _side_cast(rest[:n_side] + rest[n_side + 1:])
    length = seq // DIL_FAR
    contract_last = (((1,), (1,)), ((), ()))

    def far_block(r):
        rows = pl.ds(r, length, stride=DIL_FAR)
        q, k, v = (ref[0, rows, :].astype(BF16) for ref in (q_ref, k_ref, v_ref))
        s = lax.dot_general(q, k, contract_last, preferred_element_type=F32) + band_ref[...]
        m = jnp.max(s, axis=-1, keepdims=True)
        p = jnp.exp2(s - m)
        l = jnp.sum(p, axis=-1, keepdims=True)
        o = jnp.dot(p.astype(BF16), v, preferred_element_type=F32) / l
        ofar_ref[rows, :] = o
        lsefar_ref[rows, :] = jnp.broadcast_to(m + jnp.log2(l), o.shape)

    def near_block(ib):
        t0 = ib * DIL_TQ
        ws = jnp.clip(t0 - DIL_SPAN, 0, seq - DIL_WIN)
        off = pl.multiple_of(ws - t0 + 2 * DIL_SPAN, DIL_TQ)
        ws = pl.multiple_of(ws, DIL_TQ)
        qrows = pl.ds(pl.multiple_of(t0, DIL_TQ), DIL_TQ)
        k = k_ref[0, pl.ds(ws, DIL_WIN), :].astype(BF16)
        v = v_ref[0, pl.ds(ws, DIL_WIN), :].astype(BF16)
        s = lax.dot_general(q_ref[0, qrows, :].astype(BF16), k, contract_last,
                            preferred_element_type=F32)
        s = s + t_ref[:, pl.ds(off, DIL_WIN)]
        m = jnp.max(s, axis=-1, keepdims=True)
        p = jnp.exp2(s - m)
        l = jnp.sum(p, axis=-1, keepdims=True)
        acc = jnp.dot(p.astype(BF16), v, preferred_element_type=F32)
        lse_far = lsefar_ref[qrows, :]
        top = jnp.maximum(m, lse_far)
        w_near = jnp.exp2(m - top)
        w_far = jnp.exp2(lse_far - top)
        o_ref[0, qrows, :] = (w_near * acc + w_far * ofar_ref[qrows, :]) / (w_near * l + w_far)

    def unrolled(block):
        def body(i, carry):
            for u in range(ATTN_UNROLL):
                block(i * ATTN_UNROLL + u)
            return carry
        return body

    lax.fori_loop(0, DIL_FAR // ATTN_UNROLL, unrolled(far_block), 0)
    lax.fori_loop(0, seq // DIL_TQ // ATTN_UNROLL, unrolled(near_block), 0)


def _dil_attention(qkv, table, band, side_weights=()):
    b, _, s, _ = qkv.shape
    length = s // DIL_FAR
    assert band.shape == (length, length) and length == DIL_TQ
    side_specs, side_shapes = _side_cast_plumbing(
        side_weights, b * N_HEADS_DIL, lambda bi, h: bi * N_HEADS_DIL + h)
    head_slab = lambda first: pl.BlockSpec((None, 1, s, HEAD_DIM), lambda bi, h: (bi, first + h, 0, 0))
    return pl.pallas_call(
        functools.partial(_dil_kernel, seq=s),
        grid=(b, N_HEADS_DIL),
        in_specs=[
            head_slab(0), head_slab(N_HEADS_DIL), head_slab(2 * N_HEADS_DIL),
            pl.BlockSpec(table.shape, lambda bi, h: (0, 0)),
            pl.BlockSpec(band.shape, lambda bi, h: (0, 0)),
        ] + side_specs,
        out_specs=[pl.BlockSpec((1, s, HEAD_DIM), lambda bi, h: (bi, 0, h))] + side_specs,
        out_shape=[jax.ShapeDtypeStruct((b, s, N_HEADS_DIL * HEAD_DIM), F32)] + side_shapes,
        scratch_shapes=[pltpu.VMEM((s, HEAD_DIM), F32), pltpu.VMEM((s, HEAD_DIM), F32)],
        compiler_params=_params(("parallel", "arbitrary"), 52),
        name="dil_attn",
    )(qkv, qkv, qkv, table, band, *side_weights)


def _oproj_kernel(ona_ref, odil_ref, h_ref, gna_ref, gdil_ref, gt_ref, w_ref,
                  gn_ref, shn_ref, scn_ref, o_ref, xnn_ref):
    d_na = ona_ref.shape[2]
    na = _rms(ona_ref[0], gna_ref[...]).astype(BF16)
    nd = _rms(odil_ref[0], gdil_ref[...]).astype(BF16)
    mix = jnp.dot(na, w_ref[:d_na, :], preferred_element_type=F32)
    mix = mix + jnp.dot(nd, w_ref[d_na:, :], preferred_element_type=F32)
    o_ref[0] = h_ref[0] + gt_ref[0] * mix
    _norm_modulate_rows(o_ref.at[0], gn_ref, shn_ref, scn_ref, xnn_ref.at[0])


def _oproj(o_na, o_dil, h, g_na, g_dil, gate, w_o, next_norm):
    b, s, d = h.shape
    d_na, d_dil = o_na.shape[2], o_dil.shape[2]
    gain, shift, scale = next_norm
    tile = pl.BlockSpec((1, OPROJ_TM, d), lambda bi, i: (bi, i, 0))
    vec = pl.BlockSpec((1, 1, d), lambda bi, i: (bi, 0, 0))
    return pl.pallas_call(
        _oproj_kernel,
        grid=(b, s // OPROJ_TM),
        in_specs=[
            pl.BlockSpec((1, OPROJ_TM, d_na), lambda bi, i: (bi, i, 0)),
            pl.BlockSpec((1, OPROJ_TM, d_dil), lambda bi, i: (bi, i, 0)),
            tile,
            pl.BlockSpec((1, d_na), lambda bi, i: (0, 0)),
            pl.BlockSpec((1, d_dil), lambda bi, i: (0, 0)),
            vec,
            pl.BlockSpec((d, d), lambda bi, i: (0, 0)),
            pl.BlockSpec((1, d), lambda bi, i: (0, 0)), vec, vec,
        ],
        out_specs=[tile, tile],
        out_shape=[jax.ShapeDtypeStruct((b, s, d), F32), jax.ShapeDtypeStruct((b, s, d), BF16)],
        compiler_params=_params(("parallel", "parallel"), 52),
        name="oproj",
    )(o_na, o_dil, h, g_na.reshape(1, d_na), g_dil.reshape(1, d_dil), gate, w_o,
      gain.reshape(1, d), shift, scale)


def kernel(x, c, w_ada, b_ada, g_ffn1, w1_gate, w1_up, w1_down, g_mix, w_qkv, qn_na, kn_na, qn_dil, kn_dil, rpb_na, g_out_na, g_out_dil, w_o, g_ffn2, w2_gate, w2_up, w2_down):
    b, s, d = x.shape
    depth = w_ada.shape[0]
    rows = s // GRID_W
    rope_cos, rope_sin = _rope_tables(s)
    dil_table = _dil_near_table()
    dil_band = _dil_far_band(s // DIL_FAR)
    c_pad = jnp.pad(c, ((0, 8 - b), (0, 0)))
    lane_order = _rotary_lane_order()
    q_factor = ATTN_SCALE * LOG2_E

    h = x
    for l in range(depth):
        mod = _ada(c_pad, w_ada[l], b_ada[l])[:b]
        sh1, sc1, gt1, sh2, sc2, gt2, sh3, sc3, gt3 = [
            m.reshape(b, 1, d) for m in jnp.split(mod, N_MOD, axis=-1)]

        xn = _norm_mod(h, g_ffn1[l], sh1, sc1)
        h, xn = _ffn(xn, h, gt1,
                     w1_gate[l].astype(BF16), w1_up[l].astype(BF16), w1_down[l].astype(BF16),
                     next_norm=(g_mix[l], sh2, sc2))

        head_gains = jnp.stack([
            jnp.stack([qn_na[l] * q_factor, _take_lanes(qn_dil[l], lane_order) * q_factor]),
            jnp.stack([kn_na[l], _take_lanes(kn_dil[l], lane_order)]),
        ]).reshape(2, 2, 1, HEAD_DIM)
        qkv_na, qkv_dil = _qkv(xn, _prep_qkv_weight(w_qkv[l]), head_gains, rope_cos, rope_sin)

        o_na, w2g, w2u, wo = _na_attention(qkv_na, _na_bias(rpb_na[l], rows),
                                           side_weights=(w2_gate[l], w2_up[l], w_o[l]))
        o_dil, w2d = _dil_attention(qkv_dil, dil_table, dil_band, side_weights=(w2_down[l],))
        h, xn = _oproj(o_na, o_dil, h, g_out_na[l], g_out_dil[l], gt2, wo,
                       next_norm=(g_ffn2[l], sh3, sc3))

        h = _ffn(xn, h, gt3, w2g, w2u, w2d)
    return h
```

```python
import functools
import math

import numpy as np
import jax
import jax.numpy as jnp
from jax import lax
from jax.experimental import pallas as pl
from jax.experimental.pallas import tpu as pltpu

HEAD_DIM = 128
N_HEADS = 16
N_HEADS_NA = 8
N_HEADS_DIL = N_HEADS - N_HEADS_NA
GRID_W = 64
NA_ROWS = 8
NA_COLS = 16
DIL_PATTERNS = ((128, 1), (512, 4), (2048, 16))
ROPE_THETA = 500000.0
ROPE_DIM = HEAD_DIM // 4
N_MOD = 9
EPS = 1e-6
NEG = -1e30
ATTN_SCALE = HEAD_DIM ** -0.5
LOG2_E = math.log2(math.e)

BF16 = jnp.bfloat16
F32 = jnp.float32

MIB = 1024 * 1024
HBM_OUTPUT_VMEM_MIB = 56

ADA_TN = 1024
NORM_TM = 512
NORM_ROWS = 32
NORM_UNROLL = 4
FFN_TM = 1024
FFN_TF = 512
FFN_TF_CAST = 256
FFN_RES = 256
QKV_TM = 512
QKV_TN = 2048
QKV_PIECE = 256
SIDE_CAST_COLS = 512
PREP_ROWS = 512
OPROJ_TM = 512
ATTN_UNROLL = 4
NA_RB = 4
NA_KR = 12
DIL_TQ = 256
DIL_NEAR_PATTERNS = DIL_PATTERNS[:-1]
DIL_FAR = DIL_PATTERNS[-1][1]
DIL_FAR_STEPS = DIL_PATTERNS[-1][0] // 2 // DIL_FAR
DIL_SPAN = max(w // 2 // d * d for w, d in DIL_NEAR_PATTERNS)
DIL_WIN = DIL_TQ + 2 * DIL_SPAN


def _params(sem, vmem_mib):
    return pltpu.CompilerParams(dimension_semantics=sem, vmem_limit_bytes=vmem_mib * MIB)


def _silu(x):
    return x * (1.0 / (1.0 + jnp.exp(-x)))


def _rms(x, gain):
    ms = jnp.mean(x * x, axis=-1, keepdims=True)
    return x * lax.rsqrt(ms + EPS) * gain


def _norm_modulate_rows(x_ref, g_ref, sh_ref, sc_ref, xn_ref):
    n_rows = xn_ref.shape[0]
    gain_scale = g_ref[...] * (1.0 + sc_ref[0])
    shift = sh_ref[0]

    def body(i, carry):
        rows = pl.ds(pl.multiple_of(i * NORM_ROWS, NORM_ROWS), NORM_ROWS)
        xn_ref[rows, :] = (_rms(x_ref[rows, :], gain_scale) + shift).astype(BF16)
        return carry

    lax.fori_loop(0, n_rows // NORM_ROWS, body, 0, unroll=NORM_UNROLL)


def _norm_mod_kernel(x_ref, g_ref, sh_ref, sc_ref, xn_ref):
    _norm_modulate_rows(x_ref.at[0], g_ref, sh_ref, sc_ref, xn_ref.at[0])


def _norm_mod(x, gain, shift, scale):
    b, s, d = x.shape
    tile = pl.BlockSpec((1, NORM_TM, d), lambda bi, i: (bi, i, 0))
    vec = pl.BlockSpec((1, 1, d), lambda bi, i: (bi, 0, 0))
    return pl.pallas_call(
        _norm_mod_kernel,
        grid=(b, s // NORM_TM),
        in_specs=[tile, pl.BlockSpec((1, d), lambda bi, i: (0, 0)), vec, vec],
        out_specs=tile,
        out_shape=jax.ShapeDtypeStruct((b, s, d), BF16),
        compiler_params=_params(("parallel", "parallel"), 24),
        name="norm_mod",
    )(x, gain.reshape(1, d), shift, scale)


def _ada_kernel(c_ref, w_ref, b_ref, o_ref):
    a = _silu(c_ref[...]).astype(BF16)
    w = w_ref[...].astype(BF16)
    o_ref[...] = jnp.dot(a, w, preferred_element_type=F32) + b_ref[...]


def _ada(c_pad, w_ada, b_ada):
    rows, d = c_pad.shape
    n = w_ada.shape[1]
    return pl.pallas_call(
        _ada_kernel,
        grid=(n // ADA_TN,),
        in_specs=[
            pl.BlockSpec((rows, d), lambda j: (0, 0)),
            pl.BlockSpec((d, ADA_TN), lambda j: (0, j)),
            pl.BlockSpec((1, ADA_TN), lambda j: (0, j)),
        ],
        out_specs=pl.BlockSpec((rows, ADA_TN), lambda j: (0, j)),
        out_shape=jax.ShapeDtypeStruct((rows, n), F32),
        compiler_params=_params(("arbitrary",), 24),
        name="ada_mod",
    )(c_pad, w_ada, b_ada.reshape(1, n))


def _ffn_kernel(xn_ref, xres_ref, gt_ref, wg_ref, wu_ref, wd_ref, *rest, emit_next, cast_weights):
    rest = list(rest)
    if emit_next:
        gn_ref, shn_ref, scn_ref = rest[:3]
        rest = rest[3:]
    o_ref = rest.pop(0)
    if emit_next:
        xnn_ref = rest.pop(0)
    j = pl.program_id(1)
    n_res = o_ref.shape[2] // FFN_RES

    @pl.when(j == 0)
    def _():
        o_ref[0] = jnp.zeros(o_ref.shape[1:], F32)

    wg, wu, wd = wg_ref[...], wu_ref[...], wd_ref[...]
    if cast_weights:
        wg, wu, wd = wg.astype(BF16), wu.astype(BF16), wd.astype(BF16)
        for out_ref, w in zip(rest, (wg, wu, wd)):
            out_ref[...] = w

    xn = xn_ref[0]
    g = jnp.dot(xn, wg, preferred_element_type=F32)
    u = jnp.dot(xn, wu, preferred_element_type=F32)
    a = (_silu(g) * u).astype(BF16)
    o_ref[0] += (0.5 * gt_ref[0]) * jnp.dot(a, wd, preferred_element_type=F32)

    @pl.when(j < n_res)
    def _():
        cols = pl.ds(pl.multiple_of(j * FFN_RES, FFN_RES), FFN_RES)
        o_ref[0, :, cols] += xres_ref[0]

    if emit_next:
        @pl.when(j == pl.num_programs(1) - 1)
        def _():
            _norm_modulate_rows(o_ref.at[0], gn_ref, shn_ref, scn_ref, xnn_ref.at[0])


def _ffn(xn, x, gate, w_gate, w_up, w_down, next_norm=None, tiles=None, cast_weights=False):
    b, s, d = x.shape
    dff = w_gate.shape[1]
    per_batch = s // FFN_TM
    first, count = tiles if tiles is not None else (0, b * per_batch)
    tf = FFN_TF_CAST if cast_weights else FFN_TF
    n_steps = dff // tf
    n_res = d // FFN_RES
    assert n_res <= n_steps
    batch_of = lambda t: (first + t) // per_batch
    tile_of = lambda t: (first + t) % per_batch
    rows_mode = dict(pipeline_mode=pl.Buffered(1)) if count == 1 else {}
    vec = pl.BlockSpec((1, 1, d), lambda t, j: (batch_of(t), 0, 0))
    out_tile = pl.BlockSpec((1, FFN_TM, d), lambda t, j: (0, t, 0), **rows_mode)
    w_specs = [
        pl.BlockSpec((d, tf), lambda t, j: (0, j)),
        pl.BlockSpec((d, tf), lambda t, j: (0, j)),
        pl.BlockSpec((tf, d), lambda t, j: (j, 0)),
    ]
    in_specs = [
        pl.BlockSpec((1, FFN_TM, d), lambda t, j: (batch_of(t), tile_of(t), 0), **rows_mode),
        pl.BlockSpec((1, FFN_TM, FFN_RES),
                     lambda t, j: (batch_of(t), tile_of(t), jnp.minimum(j, n_res - 1))),
        vec,
    ] + w_specs
    args = [xn, x, gate, w_gate, w_up, w_down]
    out_specs = [out_tile]
    out_shape = [jax.ShapeDtypeStruct((1, count * FFN_TM, d), F32)]
    if next_norm is not None:
        gain, shift, scale = next_norm
        in_specs += [pl.BlockSpec((1, d), lambda t, j: (0, 0)), vec, vec]
        args += [gain.reshape(1, d), shift, scale]
        out_specs.append(out_tile)
        out_shape.append(jax.ShapeDtypeStruct((1, count * FFN_TM, d), BF16))
    if cast_weights:
        out_specs += w_specs
        out_shape += [jax.ShapeDtypeStruct(w.shape, BF16) for w in (w_gate, w_up, w_down)]
    outs = pl.pallas_call(
        functools.partial(_ffn_kernel, emit_next=next_norm is not None, cast_weights=cast_weights),
        grid=(count, n_steps),
        in_specs=in_specs,
        out_specs=out_specs,
        out_shape=out_shape,
        compiler_params=_params(("parallel", "arbitrary"), 56),
        name="ffn",
    )(*args)
    return outs if len(outs) > 1 else outs[0]


def _part_specs(parts, rows, flat_tile_of):
    specs, starts, start = [], [], 0
    for p in parts:
        n = p.shape[1] // rows
        assert n * rows == p.shape[1]
        specs.append(pl.BlockSpec(
            (1, rows, p.shape[2]),
            lambda *g, start=start, n=n: (0, jnp.clip(flat_tile_of(*g) - start, 0, n - 1), 0)))
        starts.append(start)
        start += n
    return specs, tuple(starts)


def _part_tile(refs, starts, flat_tile):
    x = refs[-1][0]
    for ref, next_start in zip(reversed(refs[:-1]), reversed(starts[1:])):
        x = jnp.where(flat_tile < next_start, ref[0], x)
    return x


def _qkv_kernel(*refs, part_starts):
    xn_refs = refs[:len(part_starts)]
    w_ref, hg_ref, cos_ref, sin_ref, o_na_ref, o_dil_ref = refs[len(part_starts):]
    flat_tile = pl.program_id(0) * pl.num_programs(1) + pl.program_id(1)
    j = pl.program_id(2)
    n_pieces = QKV_TN // QKV_PIECE
    heads_per_piece = QKV_PIECE // HEAD_DIM

    def store(head, y):
        if head < N_HEADS_NA:
            o_na_ref[head] = y.astype(BF16)
        else:
            o_dil_ref[head - N_HEADS_NA] = y

    def normed(r, head):
        return _rms(r, hg_ref[0, head // N_HEADS_NA])

    def normed_rotary(r, head):
        y = normed(r, head)
        return y * cos_ref[...] + pltpu.roll(y, HEAD_DIM // 2, 1) * sin_ref[...]

    def pieces(epilogue_of_head):
        xn = _part_tile(xn_refs, part_starts, flat_tile)
        for p in list(range(n_pieces // 2, n_pieces)) + list(range(n_pieces // 2)):
            r = jnp.dot(xn, w_ref[:, p * QKV_PIECE:(p + 1) * QKV_PIECE],
                        preferred_element_type=F32)
            for hh in range(heads_per_piece):
                head = p * heads_per_piece + hh
                store(head, epilogue_of_head(head)(r[:, hh * HEAD_DIM:(hh + 1) * HEAD_DIM], head))

    @pl.when(j < 2)
    def _():
        pieces(lambda head: normed if head < N_HEADS_NA else normed_rotary)

    @pl.when(j == 2)
    def _():
        pieces(lambda head: (lambda r, _: r))


def _qkv(xn_parts, b, s, w_qkv, head_gains, rope_cos, rope_sin):
    d, n = w_qkv.shape
    assert QKV_TN == N_HEADS * HEAD_DIM and n == 3 * QKV_TN
    tiles_per_batch = s // QKV_TM
    rope = pl.BlockSpec((QKV_TM, HEAD_DIM), lambda bi, i, j: (i, 0))
    xn_specs, part_starts = _part_specs(xn_parts, QKV_TM, lambda bi, i, j: bi * tiles_per_batch + i)
    return pl.pallas_call(
        functools.partial(_qkv_kernel, part_starts=part_starts),
        grid=(b, tiles_per_batch, n // QKV_TN),
        in_specs=xn_specs + [
            pl.BlockSpec((d, QKV_TN), lambda bi, i, j: (0, j)),
            pl.BlockSpec((1, 2, 1, HEAD_DIM), lambda bi, i, j: (jnp.minimum(j, 1), 0, 0, 0)),
            rope, rope,
        ],
        out_specs=[
            pl.BlockSpec((None, N_HEADS_NA, QKV_TM, HEAD_DIM), lambda bi, i, j: (bi, j, i, 0)),
            pl.BlockSpec((None, N_HEADS_DIL, QKV_TM, HEAD_DIM), lambda bi, i, j: (bi, j, i, 0)),
        ],
        out_shape=[
            jax.ShapeDtypeStruct((b, 3 * N_HEADS_NA, s, HEAD_DIM), BF16),
            jax.ShapeDtypeStruct((b, 3 * N_HEADS_DIL, s, HEAD_DIM), F32),
        ],
        compiler_params=_params(("parallel", "parallel", "arbitrary"), 48),
        name="qkv",
    )(*xn_parts, w_qkv, head_gains, rope_cos, rope_sin)


def _rotary_lane_order():
    half = ROPE_DIM // 2
    mid = HEAD_DIM // 2
    order = (list(range(half)) + list(range(ROPE_DIM, ROPE_DIM + mid - half))
             + list(range(half, ROPE_DIM)) + list(range(ROPE_DIM + mid - half, HEAD_DIM)))
    assert sorted(order) == list(range(HEAD_DIM))
    return order


def _take_lanes(a, order):
    runs, start = [], 0
    for i in range(1, len(order) + 1):
        if i == len(order) or order[i] != order[i - 1] + 1:
            runs.append(a[..., order[start]:order[i - 1] + 1])
            start = i
    return jnp.concatenate(runs, axis=-1)


def _rope_tables(s):
    f32 = np.float32
    pos = np.arange(s, dtype=f32)
    inv = np.power(f32(ROPE_THETA), -np.arange(0, ROPE_DIM, 2, dtype=f32) / f32(ROPE_DIM)).astype(f32)
    ang = (pos[:, None] * inv[None, :]).astype(f32)
    cos, sin = np.cos(ang).astype(f32), np.sin(ang).astype(f32)
    half = ROPE_DIM // 2
    gap = HEAD_DIM // 2 - half
    ones, zeros = np.ones((s, gap), f32), np.zeros((s, gap), f32)
    rope_cos = np.concatenate([cos, ones, cos, ones], axis=1)
    rope_sin = np.concatenate([-sin, zeros, sin, zeros], axis=1)
    return jnp.asarray(rope_cos), jnp.asarray(rope_sin)


def _lane_runs(order):
    runs, start = [], 0
    for i in range(1, len(order) + 1):
        if i == len(order) or order[i] != order[i - 1] + 1:
            runs.append((start, order[start], i - start))
            start = i
    return runs


def _prep_qkv_kernel(w_ref, o_ref):
    j = pl.program_id(1)

    @pl.when(j == 2)
    def _():
        o_ref[...] = w_ref[...].astype(BF16)

    @pl.when(j < 2)
    def _():
        lane = lax.broadcasted_iota(jnp.int32, (w_ref.shape[0], HEAD_DIM), 1)
        runs = _lane_runs(_rotary_lane_order())
        for head in range(N_HEADS):
            cols = slice(head * HEAD_DIM, (head + 1) * HEAD_DIM)
            x = w_ref[:, cols]
            if head >= N_HEADS_NA:
                y = x
                for dst, src, length in runs:
                    if dst != src:
                        moved = pltpu.roll(x, (dst - src) % HEAD_DIM, 1)
                        y = jnp.where(jnp.logical_and(lane >= dst, lane < dst + length), moved, y)
                x = y
            o_ref[:, cols] = x.astype(BF16)


def _prep_qkv_weight(w_qkv):
    d, n = w_qkv.shape
    assert n == 3 * QKV_TN
    block = pl.BlockSpec((PREP_ROWS, QKV_TN), lambda i, j: (i, j))
    return pl.pallas_call(
        _prep_qkv_kernel,
        grid=(d // PREP_ROWS, 3),
        in_specs=[block],
        out_specs=block,
        out_shape=jax.ShapeDtypeStruct((d, n), BF16),
        compiler_params=_params(("parallel", "arbitrary"), HBM_OUTPUT_VMEM_MIB),
        name="prep_qkv_weight",
    )(w_qkv)


def _na_window_start(ib, rows):
    return jnp.clip(NA_RB * ib - NA_ROWS // 2, 0, rows - NA_KR)


def _na_bias_kernel(rpb_ref, o_ref, *, rows):
    h = pl.program_id(0)
    n_ro = 2 * NA_ROWS - 1
    n_co = 2 * NA_COLS - 1
    c = lax.broadcasted_iota(jnp.int32, (GRID_W, 2 * GRID_W), 0)
    lane = lax.broadcasted_iota(jnp.int32, (GRID_W, 2 * GRID_W), 1)
    kc = lane & (GRID_W - 1)
    diff = kc - c + (NA_COLS - 1)
    cs = jnp.clip(c - NA_COLS // 2, 0, GRID_W - NA_COLS)
    col_ok = jnp.logical_and(kc >= cs, kc < cs + NA_COLS)
    neg = jnp.full((GRID_W, 2 * GRID_W), NEG, F32)
    toeplitz = []
    for ro in range(n_ro):
        t = jnp.zeros((GRID_W, 2 * GRID_W), F32)
        for m in range(n_co):
            t = jnp.where(diff == m, rpb_ref[(h * n_ro + ro) * n_co + m], t)
        toeplitz.append(jnp.where(col_ok, t * LOG2_E, neg))
    low_half = lane < GRID_W
    n_blocks = rows // NA_RB
    for var, ib in enumerate((0, n_blocks // 2, n_blocks - 1)):
        ws = min(max(NA_RB * ib - NA_ROWS // 2, 0), rows - NA_KR)
        for qr in range(NA_RB):
            r = NA_RB * ib + qr
            r_start = min(max(r - NA_ROWS // 2, 0), rows - NA_ROWS)
            for g in range(NA_KR // 2):
                pieces = []
                for kr in (2 * g, 2 * g + 1):
                    ka = ws + kr
                    inside = r_start <= ka < r_start + NA_ROWS
                    pieces.append(toeplitz[ka - r + NA_ROWS - 1] if inside else neg)
                o_ref[0, var, qr * GRID_W:(qr + 1) * GRID_W, g * 2 * GRID_W:(g + 1) * 2 * GRID_W] = (
                    jnp.where(low_half, pieces[0], pieces[1]))


def _na_bias(rpb, rows):
    nh = rpb.shape[0]
    return pl.pallas_call(
        functools.partial(_na_bias_kernel, rows=rows),
        grid=(nh,),
        in_specs=[pl.BlockSpec(memory_space=pltpu.SMEM)],
        out_specs=pl.BlockSpec((1, 3, NA_RB * GRID_W, NA_KR * GRID_W), lambda h: (h, 0, 0, 0)),
        out_shape=jax.ShapeDtypeStruct((nh, 3, NA_RB * GRID_W, NA_KR * GRID_W), F32),
        compiler_params=_params(("arbitrary",), HBM_OUTPUT_VMEM_MIB),
        name="na_bias",
    )(rpb.reshape(-1))


def _softmax_pv(s2, v):
    m = jnp.max(s2, axis=-1, keepdims=True)
    p = jnp.exp2(s2 - m)
    l = jnp.sum(p, axis=-1, keepdims=True)
    return jnp.dot(p.astype(BF16), v, preferred_element_type=F32) / l


def _side_cast(refs):
    n = len(refs) // 2
    for src_ref, dst_ref in zip(refs[:n], refs[n:]):
        for c0 in range(0, src_ref.shape[1], SIDE_CAST_COLS):
            cols = slice(c0, c0 + SIDE_CAST_COLS)
            dst_ref[:, cols] = src_ref[:, cols].astype(BF16)


def _side_cast_plumbing(weights, n_steps, step_of):
    specs, shapes = [], []
    for w in weights:
        slab = w.shape[0] // n_steps
        assert slab * n_steps == w.shape[0] and w.shape[1] % SIDE_CAST_COLS == 0
        specs.append(pl.BlockSpec((slab, w.shape[1]), lambda *g: (step_of(*g), 0)))
        shapes.append(jax.ShapeDtypeStruct(w.shape, BF16))
    return specs, shapes


def _na_kernel(q_ref, k_ref, v_ref, bias_ref, *rest, rows):
    n_side = (len(rest) - 1) // 2
    o_ref = rest[n_side]
    _side_cast(rest[:n_side] + rest[n_side + 1:])
    n_blocks = rows // NA_RB
    tq = NA_RB * GRID_W

    def block(ib):
        start = pl.multiple_of(_na_window_start(ib, rows) * GRID_W, GRID_W)
        variant = jnp.where(ib == 0, 0, jnp.where(ib == n_blocks - 1, 2, 1))
        qrows = pl.ds(pl.multiple_of(ib * tq, tq), tq)
        k = k_ref[0, pl.ds(start, NA_KR * GRID_W), :]
        v = v_ref[0, pl.ds(start, NA_KR * GRID_W), :]
        s = lax.dot_general(q_ref[0, qrows, :], k, (((1,), (1,)), ((), ())),
                            preferred_element_type=F32)
        o_ref[0, qrows, :] = _softmax_pv(s + bias_ref[0, variant], v)

    def body(i, carry):
        for u in range(ATTN_UNROLL):
            block(i * ATTN_UNROLL + u)
        return carry

    lax.fori_loop(0, n_blocks // ATTN_UNROLL, body, 0)


def _na_attention(qkv, bias, side_weights=()):
    b, _, s, _ = qkv.shape
    rows = s // GRID_W
    heads_total = N_HEADS_NA
    tq = NA_RB * GRID_W
    side_specs, side_shapes = _side_cast_plumbing(
        side_weights, b * N_HEADS_NA, lambda bi, h: bi * N_HEADS_NA + h)
    head_slab = lambda first: pl.BlockSpec((None, 1, s, HEAD_DIM), lambda bi, h: (bi, first + h, 0, 0))
    return pl.pallas_call(
        functools.partial(_na_kernel, rows=rows),
        grid=(b, N_HEADS_NA),
        in_specs=[
            head_slab(0), head_slab(heads_total), head_slab(2 * heads_total),
            pl.BlockSpec((1, 3, tq, NA_KR * GRID_W), lambda bi, h: (h, 0, 0, 0)),
        ] + side_specs,
        out_specs=[pl.BlockSpec((1, s, HEAD_DIM), lambda bi, h: (bi, 0, h))] + side_specs,
        out_shape=[jax.ShapeDtypeStruct((b, s, N_HEADS_NA * HEAD_DIM), F32)] + side_shapes,
        compiler_params=_params(("parallel", "arbitrary"), 52),
        name="na_attn",
    )(qkv, qkv, qkv, bias, *side_weights)


def _dil_near_table():
    q = np.arange(DIL_TQ)[:, None]
    x = np.arange(DIL_WIN + 2 * DIL_SPAN)[None, :]
    d = x - 2 * DIL_SPAN - q
    mult = np.zeros(d.shape, np.int64)
    for window, dil in DIL_NEAR_PATTERNS:
        reach = (window // 2 // dil) * dil
        mult += ((d % dil == 0) & (np.abs(d) <= reach)).astype(np.int64)
    table = np.where(mult > 0, np.log2(np.maximum(mult, 1)), NEG)
    return jnp.asarray(table, F32)


def _dil_far_band(length):
    i = np.arange(length)
    inside = np.abs(i[:, None] - i[None, :]) <= DIL_FAR_STEPS
    return jnp.asarray(np.where(inside, 0.0, NEG), F32)


def _dil_kernel(q_ref, k_ref, v_ref, t_ref, band_ref, *rest, seq):
    ofar_ref, lsefar_ref = rest[-2:]
    rest = rest[:-2]
    n_side = (len(rest) - 1) // 2
    o_ref = rest[n_side]
    _side_cast(rest[:n_side] + rest[n_side + 1:])
    length = seq // DIL_FAR
    contract_last = (((1,), (1,)), ((), ()))

    def far_block(r):
        rows = pl.ds(r, length, stride=DIL_FAR)
        q, k, v = (ref[0, rows, :].astype(BF16) for ref in (q_ref, k_ref, v_ref))
        s = lax.dot_general(q, k, contract_last, preferred_element_type=F32) + band_ref[...]
        m = jnp.max(s, axis=-1, keepdims=True)
        p = jnp.exp2(s - m)
        l = jnp.sum(p, axis=-1, keepdims=True)
        o = jnp.dot(p.astype(BF16), v, preferred_element_type=F32) / l
        ofar_ref[rows, :] = o
        lsefar_ref[rows, :] = jnp.broadcast_to(m + jnp.log2(l), o.shape)

    def near_block(ib):
        t0 = ib * DIL_TQ
        ws = jnp.clip(t0 - DIL_SPAN, 0, seq - DIL_WIN)
        off = pl.multiple_of(ws - t0 + 2 * DIL_SPAN, DIL_TQ)
        ws = pl.multiple_of(ws, DIL_TQ)
        qrows = pl.ds(pl.multiple_of(t0, DIL_TQ), DIL_TQ)
        k = k_ref[0, pl.ds(ws, DIL_WIN), :].astype(BF16)
        v = v_ref[0, pl.ds(ws, DIL_WIN), :].astype(BF16)
        s = lax.dot_general(q_ref[0, qrows, :].astype(BF16), k, contract_last,
                            preferred_element_type=F32)
        s = s + t_ref[:, pl.ds(off, DIL_WIN)]
        m = jnp.max(s, axis=-1, keepdims=True)
        p = jnp.exp2(s - m)
        l = jnp.sum(p, axis=-1, keepdims=True)
        acc = jnp.dot(p.astype(BF16), v, preferred_element_type=F32)
        lse_far = lsefar_ref[qrows, :]
        top = jnp.maximum(m, lse_far)
        w_near = jnp.exp2(m - top)
        w_far = jnp.exp2(lse_far - top)
        o_ref[0, qrows, :] = (w_near * acc + w_far * ofar_ref[qrows, :]) / (w_near * l + w_far)

    def unrolled(block):
        def body(i, carry):
            for u in range(ATTN_UNROLL):
                block(i * ATTN_UNROLL + u)
            return carry
        return body

    lax.fori_loop(0, DIL_FAR // ATTN_UNROLL, unrolled(far_block), 0)
    lax.fori_loop(0, seq // DIL_TQ // ATTN_UNROLL, unrolled(near_block), 0)


def _dil_attention(qkv, table, band, side_weights=()):
    b, _, s, _ = qkv.shape
    length = s // DIL_FAR
    assert band.shape == (length, length) and length == DIL_TQ
    side_specs, side_shapes = _side_cast_plumbing(
        side_weights, b * N_HEADS_DIL, lambda bi, h: bi * N_HEADS_DIL + h)
    head_slab = lambda first: pl.BlockSpec((None, 1, s, HEAD_DIM), lambda bi, h: (bi, first + h, 0, 0))
    return pl.pallas_call(
        functools.partial(_dil_kernel, seq=s),
        grid=(b, N_HEADS_DIL),
        in_specs=[
            head_slab(0), head_slab(N_HEADS_DIL), head_slab(2 * N_HEADS_DIL),
            pl.BlockSpec(table.shape, lambda bi, h: (0, 0)),
            pl.BlockSpec(band.shape, lambda bi, h: (0, 0)),
        ] + side_specs,
        out_specs=[pl.BlockSpec((1, s, HEAD_DIM), lambda bi, h: (bi, 0, h))] + side_specs,
        out_shape=[jax.ShapeDtypeStruct((b, s, N_HEADS_DIL * HEAD_DIM), F32)] + side_shapes,
        scratch_shapes=[pltpu.VMEM((s, HEAD_DIM), F32), pltpu.VMEM((s, HEAD_DIM), F32)],
        compiler_params=_params(("parallel", "arbitrary"), 52),
        name="dil_attn",
    )(qkv, qkv, qkv, table, band, *side_weights)


def _oproj_kernel(ona_ref, odil_ref, *refs, part_starts):
    h_refs = refs[:len(part_starts)]
    gna_ref, gdil_ref, gt_ref, w_ref, gn_ref, shn_ref, scn_ref, o_ref, xnn_ref = refs[len(part_starts):]
    flat_tile = pl.program_id(0) * pl.num_programs(1) + pl.program_id(1)
    d_na = ona_ref.shape[2]
    na = _rms(ona_ref[0], gna_ref[...]).astype(BF16)
    nd = _rms(odil_ref[0], gdil_ref[...]).astype(BF16)
    mix = jnp.dot(na, w_ref[:d_na, :], preferred_element_type=F32)
    mix = mix + jnp.dot(nd, w_ref[d_na:, :], preferred_element_type=F32)
    o_ref[0] = _part_tile(h_refs, part_starts, flat_tile) + gt_ref[0] * mix
    _norm_modulate_rows(o_ref.at[0], gn_ref, shn_ref, scn_ref, xnn_ref.at[0])


def _oproj(o_na, o_dil, h_parts, g_na, g_dil, gate, w_o, next_norm):
    b, s, d_na = o_na.shape
    d_dil = o_dil.shape[2]
    d = w_o.shape[1]
    gain, shift, scale = next_norm
    tiles_per_batch = s // OPROJ_TM
    tile = pl.BlockSpec((1, OPROJ_TM, d), lambda bi, i: (bi, i, 0))
    vec = pl.BlockSpec((1, 1, d), lambda bi, i: (bi, 0, 0))
    h_specs, part_starts = _part_specs(h_parts, OPROJ_TM, lambda bi, i: bi * tiles_per_batch + i)
    return pl.pallas_call(
        functools.partial(_oproj_kernel, part_starts=part_starts),
        grid=(b, tiles_per_batch),
        in_specs=[
            pl.BlockSpec((1, OPROJ_TM, d_na), lambda bi, i: (bi, i, 0)),
            pl.BlockSpec((1, OPROJ_TM, d_dil), lambda bi, i: (bi, i, 0)),
        ] + h_specs + [
            pl.BlockSpec((1, d_na), lambda bi, i: (0, 0)),
            pl.BlockSpec((1, d_dil), lambda bi, i: (0, 0)),
            vec,
            pl.BlockSpec((d, d), lambda bi, i: (0, 0)),
            pl.BlockSpec((1, d), lambda bi, i: (0, 0)), vec, vec,
        ],
        out_specs=[tile, tile],
        out_shape=[jax.ShapeDtypeStruct((b, s, d), F32), jax.ShapeDtypeStruct((b, s, d), BF16)],
        compiler_params=_params(("parallel", "parallel"), 56),
        name="oproj",
    )(o_na, o_dil, *h_parts, g_na.reshape(1, d_na), g_dil.reshape(1, d_dil), gate, w_o,
      gain.reshape(1, d), shift, scale)


def kernel(x, c, w_ada, b_ada, g_ffn1, w1_gate, w1_up, w1_down, g_mix, w_qkv, qn_na, kn_na, qn_dil, kn_dil, rpb_na, g_out_na, g_out_dil, w_o, g_ffn2, w2_gate, w2_up, w2_down):
    b, s, d = x.shape
    depth = w_ada.shape[0]
    rows = s // GRID_W
    rope_cos, rope_sin = _rope_tables(s)
    dil_table = _dil_near_table()
    dil_band = _dil_far_band(s // DIL_FAR)
    c_pad = jnp.pad(c, ((0, 8 - b), (0, 0)))
    lane_order = _rotary_lane_order()
    q_factor = ATTN_SCALE * LOG2_E

    h = x
    for l in range(depth):
        mod = _ada(c_pad, w_ada[l], b_ada[l])[:b]
        sh1, sc1, gt1, sh2, sc2, gt2, sh3, sc3, gt3 = [
            m.reshape(b, 1, d) for m in jnp.split(mod, N_MOD, axis=-1)]

        xn = _norm_mod(h, g_ffn1[l], sh1, sc1)
        n_tiles = b * s // FFN_TM
        h_a, xn_a, w1g, w1u, w1d = _ffn(xn, h, gt1, w1_gate[l], w1_up[l], w1_down[l],
                                        next_norm=(g_mix[l], sh2, sc2),
                                        tiles=(0, 1), cast_weights=True)
        h_b, xn_b = _ffn(xn, h, gt1, w1g, w1u, w1d, next_norm=(g_mix[l], sh2, sc2),
                         tiles=(1, n_tiles - 1))

        head_gains = jnp.stack([
            jnp.stack([qn_na[l] * q_factor, _take_lanes(qn_dil[l], lane_order) * q_factor]),
            jnp.stack([kn_na[l], _take_lanes(kn_dil[l], lane_order)]),
        ]).reshape(2, 2, 1, HEAD_DIM)
        qkv_na, qkv_dil = _qkv((xn_a, xn_b), b, s, _prep_qkv_weight(w_qkv[l]), head_gains,
                               rope_cos, rope_sin)

        o_na, w2g, w2u, wo = _na_attention(qkv_na, _na_bias(rpb_na[l], rows),
                                           side_weights=(w2_gate[l], w2_up[l], w_o[l]))
        o_dil, w2d = _dil_attention(qkv_dil, dil_table, dil_band, side_weights=(w2_down[l],))
        h, xn = _oproj(o_na, o_dil, (h_a, h_b), g_out_na[l], g_out_dil[l], gt2, wo,
                       next_norm=(g_ffn2[l], sh3, sc3))

        h = _ffn(xn, h, gt3, w2g, w2u, w2d).reshape(b, s, d)
    return h
```

```python
import functools
import math

import numpy as np
import jax
import jax.numpy as jnp
from jax import lax
from jax.experimental import pallas as pl
from jax.experimental.pallas import tpu as pltpu

HEAD_DIM = 128
N_HEADS = 16
N_HEADS_NA = 8
N_HEADS_DIL = N_HEADS - N_HEADS_NA
GRID_W = 64
NA_ROWS = 8
NA_COLS = 16
DIL_PATTERNS = ((128, 1), (512, 4), (2048, 16))
ROPE_THETA = 500000.0
ROPE_DIM = HEAD_DIM // 4
N_MOD = 9
EPS = 1e-6
NEG = -1e30
ATTN_SCALE = HEAD_DIM ** -0.5
LOG2_E = math.log2(math.e)

BF16 = jnp.bfloat16
F32 = jnp.float32

MIB = 1024 * 1024
HBM_OUTPUT_VMEM_MIB = 56

ADA_TN = 1024
NORM_TM = 512
NORM_ROWS = 32
NORM_UNROLL = 4
FFN_TM = 1024
FFN_TF = 512
FFN_TF_CAST = 256
FFN_RES = 256
QKV_TM = 512
QKV_TN = 2048
QKV_PIECE = 256
SIDE_CAST_COLS = 512
PREP_ROWS = 512
OPROJ_TM = 512
ATTN_UNROLL = 4
NA_RB = 4
NA_KR = 12
DIL_TQ = 256
DIL_NEAR_PATTERNS = DIL_PATTERNS[:-1]
DIL_FAR = DIL_PATTERNS[-1][1]
DIL_FAR_STEPS = DIL_PATTERNS[-1][0] // 2 // DIL_FAR
DIL_SPAN = max(w // 2 // d * d for w, d in DIL_NEAR_PATTERNS)
DIL_WIN = DIL_TQ + 2 * DIL_SPAN


def _params(sem, vmem_mib):
    return pltpu.CompilerParams(dimension_semantics=sem, vmem_limit_bytes=vmem_mib * MIB)


def _silu(x):
    return x * (1.0 / (1.0 + jnp.exp(-x)))


def _rms(x, gain):
    ms = jnp.mean(x * x, axis=-1, keepdims=True)
    return x * lax.rsqrt(ms + EPS) * gain


def _norm_modulate_rows(x_ref, g_ref, sh_ref, sc_ref, xn_ref):
    n_rows = xn_ref.shape[0]
    gain_scale = g_ref[...] * (1.0 + sc_ref[0])
    shift = sh_ref[0]

    def body(i, carry):
        rows = pl.ds(pl.multiple_of(i * NORM_ROWS, NORM_ROWS), NORM_ROWS)
        xn_ref[rows, :] = (_rms(x_ref[rows, :], gain_scale) + shift).astype(BF16)
        return carry

    lax.fori_loop(0, n_rows // NORM_ROWS, body, 0, unroll=NORM_UNROLL)


def _norm_mod_kernel(x_ref, g_ref, sh_ref, sc_ref, xn_ref):
    _norm_modulate_rows(x_ref.at[0], g_ref, sh_ref, sc_ref, xn_ref.at[0])


def _norm_mod(x, gain, shift, scale):
    b, s, d = x.shape
    tile = pl.BlockSpec((1, NORM_TM, d), lambda bi, i: (bi, i, 0))
    vec = pl.BlockSpec((1, 1, d), lambda bi, i: (bi, 0, 0))
    return pl.pallas_call(
        _norm_mod_kernel,
        grid=(b, s // NORM_TM),
        in_specs=[tile, pl.BlockSpec((1, d), lambda bi, i: (0, 0)), vec, vec],
        out_specs=tile,
        out_shape=jax.ShapeDtypeStruct((b, s, d), BF16),
        compiler_params=_params(("parallel", "parallel"), 24),
        name="norm_mod",
    )(x, gain.reshape(1, d), shift, scale)


def _ada_kernel(c_ref, w_ref, b_ref, o_ref):
    a = _silu(c_ref[...]).astype(BF16)
    w = w_ref[...].astype(BF16)
    o_ref[...] = jnp.dot(a, w, preferred_element_type=F32) + b_ref[...]


def _ada(c_pad, w_ada, b_ada):
    rows, d = c_pad.shape
    n = w_ada.shape[1]
    return pl.pallas_call(
        _ada_kernel,
        grid=(n // ADA_TN,),
        in_specs=[
            pl.BlockSpec((rows, d), lambda j: (0, 0)),
            pl.BlockSpec((d, ADA_TN), lambda j: (0, j)),
            pl.BlockSpec((1, ADA_TN), lambda j: (0, j)),
        ],
        out_specs=pl.BlockSpec((rows, ADA_TN), lambda j: (0, j)),
        out_shape=jax.ShapeDtypeStruct((rows, n), F32),
        compiler_params=_params(("arbitrary",), 24),
        name="ada_mod",
    )(c_pad, w_ada, b_ada.reshape(1, n))


def _ffn_kernel(xn_ref, xres_ref, gt_ref, wg_ref, wu_ref, wd_ref, *rest, emit_next, cast_weights):
    rest = list(rest)
    if emit_next:
        gn_ref, shn_ref, scn_ref = rest[:3]
        rest = rest[3:]
    o_ref = rest.pop(0)
    if emit_next:
        xnn_ref = rest.pop(0)
    j = pl.program_id(1)
    n_res = o_ref.shape[2] // FFN_RES

    @pl.when(j == 0)
    def _():
        o_ref[0] = jnp.zeros(o_ref.shape[1:], F32)

    wg, wu, wd = wg_ref[...], wu_ref[...], wd_ref[...]
    if cast_weights:
        wg, wu, wd = wg.astype(BF16), wu.astype(BF16), wd.astype(BF16)
        for out_ref, w in zip(rest, (wg, wu, wd)):
            out_ref[...] = w

    xn = xn_ref[0]
    g = jnp.dot(xn, wg, preferred_element_type=F32)
    u = jnp.dot(xn, wu, preferred_element_type=F32)
    a = (_silu(g) * u).astype(BF16)
    o_ref[0] += (0.5 * gt_ref[0]) * jnp.dot(a, wd, preferred_element_type=F32)

    @pl.when(j < n_res)
    def _():
        cols = pl.ds(pl.multiple_of(j * FFN_RES, FFN_RES), FFN_RES)
        o_ref[0, :, cols] += xres_ref[0]

    if emit_next:
        @pl.when(j == pl.num_programs(1) - 1)
        def _():
            _norm_modulate_rows(o_ref.at[0], gn_ref, shn_ref, scn_ref, xnn_ref.at[0])


def _ffn(xn, x, gate, w_gate, w_up, w_down, next_norm=None, tiles=None, cast_weights=False):
    b, s, d = x.shape
    dff = w_gate.shape[1]
    per_batch = s // FFN_TM
    first, count = tiles if tiles is not None else (0, b * per_batch)
    tf = FFN_TF_CAST if cast_weights else FFN_TF
    n_steps = dff // tf
    n_res = d // FFN_RES
    assert n_res <= n_steps
    batch_of = lambda t: (first + t) // per_batch
    tile_of = lambda t: (first + t) % per_batch
    rows_mode = dict(pipeline_mode=pl.Buffered(1)) if count == 1 else {}
    vec = pl.BlockSpec((1, 1, d), lambda t, j: (batch_of(t), 0, 0))
    out_tile = pl.BlockSpec((1, FFN_TM, d), lambda t, j: (0, t, 0), **rows_mode)
    w_specs = [
        pl.BlockSpec((d, tf), lambda t, j: (0, j)),
        pl.BlockSpec((d, tf), lambda t, j: (0, j)),
        pl.BlockSpec((tf, d), lambda t, j: (j, 0)),
    ]
    in_specs = [
        pl.BlockSpec((1, FFN_TM, d), lambda t, j: (batch_of(t), tile_of(t), 0), **rows_mode),
        pl.BlockSpec((1, FFN_TM, FFN_RES),
                     lambda t, j: (batch_of(t), tile_of(t), jnp.minimum(j, n_res - 1))),
        vec,
    ] + w_specs
    args = [xn, x, gate, w_gate, w_up, w_down]
    out_specs = [out_tile]
    out_shape = [jax.ShapeDtypeStruct((1, count * FFN_TM, d), F32)]
    if next_norm is not None:
        gain, shift, scale = next_norm
        in_specs += [pl.BlockSpec((1, d), lambda t, j: (0, 0)), vec, vec]
        args += [gain.reshape(1, d), shift, scale]
        out_specs.append(out_tile)
        out_shape.append(jax.ShapeDtypeStruct((1, count * FFN_TM, d), BF16))
    if cast_weights:
        out_specs += w_specs
        out_shape += [jax.ShapeDtypeStruct(w.shape, BF16) for w in (w_gate, w_up, w_down)]
    outs = pl.pallas_call(
        functools.partial(_ffn_kernel, emit_next=next_norm is not None, cast_weights=cast_weights),
        grid=(count, n_steps),
        in_specs=in_specs,
        out_specs=out_specs,
        out_shape=out_shape,
        compiler_params=_params(("parallel", "arbitrary"), 56),
        name="ffn",
    )(*args)
    return outs if len(outs) > 1 else outs[0]


def _part_specs(parts, rows, flat_tile_of):
    specs, starts, start = [], [], 0
    for p in parts:
        n = p.shape[1] // rows
        assert n * rows == p.shape[1]
        specs.append(pl.BlockSpec(
            (1, rows, p.shape[2]),
            lambda *g, start=start, n=n: (0, jnp.clip(flat_tile_of(*g) - start, 0, n - 1), 0)))
        starts.append(start)
        start += n
    return specs, tuple(starts)


def _part_tile(refs, starts, flat_tile):
    x = refs[-1][0]
    for ref, next_start in zip(reversed(refs[:-1]), reversed(starts[1:])):
        x = jnp.where(flat_tile < next_start, ref[0], x)
    return x


def _qkv_kernel(*refs, part_starts):
    xn_refs = refs[:len(part_starts)]
    w_ref, hg_ref, cos_ref, sin_ref, o_na_ref, o_dil_ref = refs[len(part_starts):]
    flat_tile = pl.program_id(0) * pl.num_programs(1) + pl.program_id(1)
    j = pl.program_id(2)
    n_pieces = QKV_TN // QKV_PIECE
    heads_per_piece = QKV_PIECE // HEAD_DIM

    def store(head, y):
        if head < N_HEADS_NA:
            o_na_ref[head] = y.astype(BF16)
        else:
            o_dil_ref[head - N_HEADS_NA] = y

    def normed(r, head):
        return _rms(r, hg_ref[0, head // N_HEADS_NA])

    def normed_rotary(r, head):
        y = normed(r, head)
        return y * cos_ref[...] + pltpu.roll(y, HEAD_DIM // 2, 1) * sin_ref[...]

    def pieces(epilogue_of_head):
        xn = _part_tile(xn_refs, part_starts, flat_tile)
        for p in list(range(n_pieces // 2, n_pieces)) + list(range(n_pieces // 2)):
            r = jnp.dot(xn, w_ref[:, p * QKV_PIECE:(p + 1) * QKV_PIECE],
                        preferred_element_type=F32)
            for hh in range(heads_per_piece):
                head = p * heads_per_piece + hh
                store(head, epilogue_of_head(head)(r[:, hh * HEAD_DIM:(hh + 1) * HEAD_DIM], head))

    @pl.when(j < 2)
    def _():
        pieces(lambda head: normed if head < N_HEADS_NA else normed_rotary)

    @pl.when(j == 2)
    def _():
        pieces(lambda head: (lambda r, _: r))


def _qkv(xn_parts, b, s, w_qkv, head_gains, rope_cos, rope_sin):
    d, n = w_qkv.shape
    assert QKV_TN == N_HEADS * HEAD_DIM and n == 3 * QKV_TN
    tiles_per_batch = s // QKV_TM
    rope = pl.BlockSpec((QKV_TM, HEAD_DIM), lambda bi, i, j: (i, 0))
    xn_specs, part_starts = _part_specs(xn_parts, QKV_TM, lambda bi, i, j: bi * tiles_per_batch + i)
    return pl.pallas_call(
        functools.partial(_qkv_kernel, part_starts=part_starts),
        grid=(b, tiles_per_batch, n // QKV_TN),
        in_specs=xn_specs + [
            pl.BlockSpec((d, QKV_TN), lambda bi, i, j: (0, j)),
            pl.BlockSpec((1, 2, 1, HEAD_DIM), lambda bi, i, j: (jnp.minimum(j, 1), 0, 0, 0)),
            rope, rope,
        ],
        out_specs=[
            pl.BlockSpec((None, N_HEADS_NA, QKV_TM, HEAD_DIM), lambda bi, i, j: (bi, j, i, 0)),
            pl.BlockSpec((None, N_HEADS_DIL, QKV_TM, HEAD_DIM), lambda bi, i, j: (bi, j, i, 0)),
        ],
        out_shape=[
            jax.ShapeDtypeStruct((b, 3 * N_HEADS_NA, s, HEAD_DIM), BF16),
            jax.ShapeDtypeStruct((b, 3 * N_HEADS_DIL, s, HEAD_DIM), F32),
        ],
        compiler_params=_params(("parallel", "parallel", "arbitrary"), 48),
        name="qkv",
    )(*xn_parts, w_qkv, head_gains, rope_cos, rope_sin)


def _rotary_lane_order():
    half = ROPE_DIM // 2
    mid = HEAD_DIM // 2
    order = (list(range(half)) + list(range(ROPE_DIM, ROPE_DIM + mid - half))
             + list(range(half, ROPE_DIM)) + list(range(ROPE_DIM + mid - half, HEAD_DIM)))
    assert sorted(order) == list(range(HEAD_DIM))
    return order


def _take_lanes(a, order):
    runs, start = [], 0
    for i in range(1, len(order) + 1):
        if i == len(order) or order[i] != order[i - 1] + 1:
            runs.append(a[..., order[start]:order[i - 1] + 1])
            start = i
    return jnp.concatenate(runs, axis=-1)


def _rope_tables(s):
    f32 = np.float32
    pos = np.arange(s, dtype=f32)
    inv = np.power(f32(ROPE_THETA), -np.arange(0, ROPE_DIM, 2, dtype=f32) / f32(ROPE_DIM)).astype(f32)
    ang = (pos[:, None] * inv[None, :]).astype(f32)
    cos, sin = np.cos(ang).astype(f32), np.sin(ang).astype(f32)
    half = ROPE_DIM // 2
    gap = HEAD_DIM // 2 - half
    ones, zeros = np.ones((s, gap), f32), np.zeros((s, gap), f32)
    rope_cos = np.concatenate([cos, ones, cos, ones], axis=1)
    rope_sin = np.concatenate([-sin, zeros, sin, zeros], axis=1)
    return jnp.asarray(rope_cos), jnp.asarray(rope_sin)


def _lane_runs(order):
    runs, start = [], 0
    for i in range(1, len(order) + 1):
        if i == len(order) or order[i] != order[i - 1] + 1:
            runs.append((start, order[start], i - start))
            start = i
    return runs


def _prep_qkv_kernel(w_ref, o_ref):
    j = pl.program_id(1)

    @pl.when(j == 2)
    def _():
        o_ref[...] = w_ref[...].astype(BF16)

    @pl.when(j < 2)
    def _():
        lane = lax.broadcasted_iota(jnp.int32, (w_ref.shape[0], HEAD_DIM), 1)
        runs = _lane_runs(_rotary_lane_order())
        for head in range(N_HEADS):
            cols = slice(head * HEAD_DIM, (head + 1) * HEAD_DIM)
            x = w_ref[:, cols]
            if head >= N_HEADS_NA:
                y = x
                for dst, src, length in runs:
                    if dst != src:
                        moved = pltpu.roll(x, (dst - src) % HEAD_DIM, 1)
                        y = jnp.where(jnp.logical_and(lane >= dst, lane < dst + length), moved, y)
                x = y
            o_ref[:, cols] = x.astype(BF16)


def _prep_qkv_weight(w_qkv):
    d, n = w_qkv.shape
    assert n == 3 * QKV_TN
    block = pl.BlockSpec((PREP_ROWS, QKV_TN), lambda i, j: (i, j))
    return pl.pallas_call(
        _prep_qkv_kernel,
        grid=(d // PREP_ROWS, 3),
        in_specs=[block],
        out_specs=block,
        out_shape=jax.ShapeDtypeStruct((d, n), BF16),
        compiler_params=_params(("parallel", "arbitrary"), HBM_OUTPUT_VMEM_MIB),
        name="prep_qkv_weight",
    )(w_qkv)


def _na_window_start(ib, rows):
    return jnp.clip(NA_RB * ib - NA_ROWS // 2, 0, rows - NA_KR)


def _na_bias_kernel(rpb_ref, o_ref, *, rows):
    n_ro = 2 * NA_ROWS - 1
    lanes = 2 * GRID_W
    assert lanes == rpb_ref.shape[2]
    c = lax.broadcasted_iota(jnp.int32, (GRID_W, lanes), 0)
    lane = lax.broadcasted_iota(jnp.int32, (GRID_W, lanes), 1)
    kc = lane & (GRID_W - 1)
    cs = jnp.clip(c - NA_COLS // 2, 0, GRID_W - NA_COLS)
    col_ok = jnp.logical_and(kc >= cs, kc < cs + NA_COLS)
    low_half = lane < GRID_W
    neg = jnp.full((GRID_W, lanes), NEG, F32)
    toeplitz = []
    for ro in range(n_ro):
        row = jnp.broadcast_to(rpb_ref[0, ro:ro + 1, :], (GRID_W, lanes))
        shift = lanes - (NA_COLS - 1)
        lo = pltpu.roll(row, shift, 1, stride=1, stride_axis=0)
        hi = pltpu.roll(row, (shift + GRID_W) % lanes, 1, stride=1, stride_axis=0)
        toeplitz.append(jnp.where(col_ok, jnp.where(low_half, lo, hi) * LOG2_E, neg))
    n_blocks = rows // NA_RB
    for var, ib in enumerate((0, n_blocks // 2, n_blocks - 1)):
        ws = min(max(NA_RB * ib - NA_ROWS // 2, 0), rows - NA_KR)
        for qr in range(NA_RB):
            r = NA_RB * ib + qr
            r_start = min(max(r - NA_ROWS // 2, 0), rows - NA_ROWS)
            for g in range(NA_KR // 2):
                pieces = []
                for kr in (2 * g, 2 * g + 1):
                    ka = ws + kr
                    inside = r_start <= ka < r_start + NA_ROWS
                    pieces.append(toeplitz[ka - r + NA_ROWS - 1] if inside else neg)
                o_ref[0, var, qr * GRID_W:(qr + 1) * GRID_W, g * 2 * GRID_W:(g + 1) * 2 * GRID_W] = (
                    jnp.where(low_half, pieces[0], pieces[1]))


def _na_bias(rpb, rows):
    nh, n_ro, n_co = rpb.shape
    lanes = 2 * GRID_W
    rpb_rows = jnp.pad(rpb, ((0, 0), (0, 0), (0, lanes - n_co)))
    return pl.pallas_call(
        functools.partial(_na_bias_kernel, rows=rows),
        grid=(nh,),
        in_specs=[pl.BlockSpec((1, n_ro, lanes), lambda h: (h, 0, 0))],
        out_specs=pl.BlockSpec((1, 3, NA_RB * GRID_W, NA_KR * GRID_W), lambda h: (h, 0, 0, 0)),
        out_shape=jax.ShapeDtypeStruct((nh, 3, NA_RB * GRID_W, NA_KR * GRID_W), F32),
        compiler_params=_params(("arbitrary",), HBM_OUTPUT_VMEM_MIB),
        name="na_bias",
    )(rpb_rows)


def _softmax_pv(s2, v):
    m = jnp.max(s2, axis=-1, keepdims=True)
    p = jnp.exp2(s2 - m)
    l = jnp.sum(p, axis=-1, keepdims=True)
    return jnp.dot(p.astype(BF16), v, preferred_element_type=F32) / l


def _side_cast(refs):
    n = len(refs) // 2
    for src_ref, dst_ref in zip(refs[:n], refs[n:]):
        for c0 in range(0, src_ref.shape[1], SIDE_CAST_COLS):
            cols = slice(c0, c0 + SIDE_CAST_COLS)
            dst_ref[:, cols] = src_ref[:, cols].astype(BF16)


def _side_cast_plumbing(weights, n_steps, step_of):
    specs, shapes = [], []
    for w in weights:
        slab = w.shape[0] // n_steps
        assert slab * n_steps == w.shape[0] and w.shape[1] % SIDE_CAST_COLS == 0
        specs.append(pl.BlockSpec((slab, w.shape[1]), lambda *g: (step_of(*g), 0)))
        shapes.append(jax.ShapeDtypeStruct(w.shape, BF16))
    return specs, shapes


def _na_kernel(q_ref, k_ref, v_ref, bias_ref, *rest, rows):
    n_side = (len(rest) - 1) // 2
    o_ref = rest[n_side]
    _side_cast(rest[:n_side] + rest[n_side + 1:])
    n_blocks = rows // NA_RB
    tq = NA_RB * GRID_W

    def block(ib):
        start = pl.multiple_of(_na_window_start(ib, rows) * GRID_W, GRID_W)
        variant = jnp.where(ib == 0, 0, jnp.where(ib == n_blocks - 1, 2, 1))
        qrows = pl.ds(pl.multiple_of(ib * tq, tq), tq)
        k = k_ref[0, pl.ds(start, NA_KR * GRID_W), :]
        v = v_ref[0, pl.ds(start, NA_KR * GRID_W), :]
        s = lax.dot_general(q_ref[0, qrows, :], k, (((1,), (1,)), ((), ())),
                            preferred_element_type=F32)
        o_ref[0, qrows, :] = _softmax_pv(s + bias_ref[0, variant], v).astype(o_ref.dtype)

    def body(i, carry):
        for u in range(ATTN_UNROLL):
            block(i * ATTN_UNROLL + u)
        return carry

    lax.fori_loop(0, n_blocks // ATTN_UNROLL, body, 0)


def _na_attention(qkv, bias, side_weights=()):
    b, _, s, _ = qkv.shape
    rows = s // GRID_W
    heads_total = N_HEADS_NA
    tq = NA_RB * GRID_W
    side_specs, side_shapes = _side_cast_plumbing(
        side_weights, b * N_HEADS_NA, lambda bi, h: bi * N_HEADS_NA + h)
    head_slab = lambda first: pl.BlockSpec((None, 1, s, HEAD_DIM), lambda bi, h: (bi, first + h, 0, 0))
    return pl.pallas_call(
        functools.partial(_na_kernel, rows=rows),
        grid=(b, N_HEADS_NA),
        in_specs=[
            head_slab(0), head_slab(heads_total), head_slab(2 * heads_total),
            pl.BlockSpec((1, 3, tq, NA_KR * GRID_W), lambda bi, h: (h, 0, 0, 0)),
        ] + side_specs,
        out_specs=[pl.BlockSpec((1, s, HEAD_DIM), lambda bi, h: (bi, 0, h))] + side_specs,
        out_shape=[jax.ShapeDtypeStruct((b, s, N_HEADS_NA * HEAD_DIM), BF16)] + side_shapes,
        compiler_params=_params(("parallel", "arbitrary"), 52),
        name="na_attn",
    )(qkv, qkv, qkv, bias, *side_weights)


def _dil_near_table():
    q = np.arange(DIL_TQ)[:, None]
    x = np.arange(DIL_WIN + 2 * DIL_SPAN)[None, :]
    d = x - 2 * DIL_SPAN - q
    mult = np.zeros(d.shape, np.int64)
    for window, dil in DIL_NEAR_PATTERNS:
        reach = (window // 2 // dil) * dil
        mult += ((d % dil == 0) & (np.abs(d) <= reach)).astype(np.int64)
    table = np.where(mult > 0, np.log2(np.maximum(mult, 1)), NEG)
    return jnp.asarray(table, F32)


def _dil_far_band(length):
    i = np.arange(length)
    inside = np.abs(i[:, None] - i[None, :]) <= DIL_FAR_STEPS
    return jnp.asarray(np.where(inside, 0.0, NEG), F32)


def _dil_kernel(q_ref, k_ref, v_ref, t_ref, band_ref, *rest, seq):
    ofar_ref, lsefar_ref = rest[-2:]
    rest = rest[:-2]
    n_side = (len(rest) - 1) // 2
    o_ref = rest[n_side]
    _side_cast(rest[:n_side] + rest[n_side + 1:])
    length = seq // DIL_FAR
    contract_last = (((1,), (1,)), ((), ()))

    def far_block(r):
        rows = pl.ds(r, length, stride=DIL_FAR)
        q, k, v = (ref[0, rows, :].astype(BF16) for ref in (q_ref, k_ref, v_ref))
        s = lax.dot_general(q, k, contract_last, preferred_element_type=F32) + band_ref[...]
        m = jnp.max(s, axis=-1, keepdims=True)
        p = jnp.exp2(s - m)
        l = jnp.sum(p, axis=-1, keepdims=True)
        o = jnp.dot(p.astype(BF16), v, preferred_element_type=F32) / l
        ofar_ref[rows, :] = o
        lsefar_ref[rows, :] = jnp.broadcast_to(m + jnp.log2(l), o.shape)

    def near_block(ib):
        t0 = ib * DIL_TQ
        ws = jnp.clip(t0 - DIL_SPAN, 0, seq - DIL_WIN)
        off = pl.multiple_of(ws - t0 + 2 * DIL_SPAN, DIL_TQ)
        ws = pl.multiple_of(ws, DIL_TQ)
        qrows = pl.ds(pl.multiple_of(t0, DIL_TQ), DIL_TQ)
        k = k_ref[0, pl.ds(ws, DIL_WIN), :].astype(BF16)
        v = v_ref[0, pl.ds(ws, DIL_WIN), :].astype(BF16)
        s = lax.dot_general(q_ref[0, qrows, :].astype(BF16), k, contract_last,
                            preferred_element_type=F32)
        s = s + t_ref[:, pl.ds(off, DIL_WIN)]
        m = jnp.max(s, axis=-1, keepdims=True)
        p = jnp.exp2(s - m)
        l = jnp.sum(p, axis=-1, keepdims=True)
        acc = jnp.dot(p.astype(BF16), v, preferred_element_type=F32)
        lse_far = lsefar_ref[qrows, :]
        top = jnp.maximum(m, lse_far)
        w_near = jnp.exp2(m - top)
        w_far = jnp.exp2(lse_far - top)
        o = (w_near * acc + w_far * ofar_ref[qrows, :]) / (w_near * l + w_far)
        o_ref[0, qrows, :] = o.astype(o_ref.dtype)

    def unrolled(block):
        def body(i, carry):
            for u in range(ATTN_UNROLL):
                block(i * ATTN_UNROLL + u)
            return carry
        return body

    lax.fori_loop(0, DIL_FAR // ATTN_UNROLL, unrolled(far_block), 0)
    lax.fori_loop(0, seq // DIL_TQ // ATTN_UNROLL, unrolled(near_block), 0)


def _dil_attention(qkv, table, band, side_weights=()):
    b, _, s, _ = qkv.shape
    length = s // DIL_FAR
    assert band.shape == (length, length) and length == DIL_TQ
    side_specs, side_shapes = _side_cast_plumbing(
        side_weights, b * N_HEADS_DIL, lambda bi, h: bi * N_HEADS_DIL + h)
    head_slab = lambda first: pl.BlockSpec((None, 1, s, HEAD_DIM), lambda bi, h: (bi, first + h, 0, 0))
    return pl.pallas_call(
        functools.partial(_dil_kernel, seq=s),
        grid=(b, N_HEADS_DIL),
        in_specs=[
            head_slab(0), head_slab(N_HEADS_DIL), head_slab(2 * N_HEADS_DIL),
            pl.BlockSpec(table.shape, lambda bi, h: (0, 0)),
            pl.BlockSpec(band.shape, lambda bi, h: (0, 0)),
        ] + side_specs,
        out_specs=[pl.BlockSpec((1, s, HEAD_DIM), lambda bi, h: (bi, 0, h))] + side_specs,
        out_shape=[jax.ShapeDtypeStruct((b, s, N_HEADS_DIL * HEAD_DIM), BF16)] + side_shapes,
        scratch_shapes=[pltpu.VMEM((s, HEAD_DIM), F32), pltpu.VMEM((s, HEAD_DIM), F32)],
        compiler_params=_params(("parallel", "arbitrary"), 52),
        name="dil_attn",
    )(qkv, qkv, qkv, table, band, *side_weights)


def _oproj_kernel(ona_ref, odil_ref, *refs, part_starts):
    h_refs = refs[:len(part_starts)]
    gna_ref, gdil_ref, gt_ref, w_ref, gn_ref, shn_ref, scn_ref, o_ref, xnn_ref = refs[len(part_starts):]
    flat_tile = pl.program_id(0) * pl.num_programs(1) + pl.program_id(1)
    d_na = ona_ref.shape[2]
    na = _rms(ona_ref[0].astype(F32), gna_ref[...]).astype(BF16)
    nd = _rms(odil_ref[0].astype(F32), gdil_ref[...]).astype(BF16)
    mix = jnp.dot(na, w_ref[:d_na, :], preferred_element_type=F32)
    mix = mix + jnp.dot(nd, w_ref[d_na:, :], preferred_element_type=F32)
    o_ref[0] = _part_tile(h_refs, part_starts, flat_tile) + gt_ref[0] * mix
    _norm_modulate_rows(o_ref.at[0], gn_ref, shn_ref, scn_ref, xnn_ref.at[0])


def _oproj(o_na, o_dil, h_parts, g_na, g_dil, gate, w_o, next_norm):
    b, s, d_na = o_na.shape
    d_dil = o_dil.shape[2]
    d = w_o.shape[1]
    gain, shift, scale = next_norm
    tiles_per_batch = s // OPROJ_TM
    tile = pl.BlockSpec((1, OPROJ_TM, d), lambda bi, i: (bi, i, 0))
    vec = pl.BlockSpec((1, 1, d), lambda bi, i: (bi, 0, 0))
    h_specs, part_starts = _part_specs(h_parts, OPROJ_TM, lambda bi, i: bi * tiles_per_batch + i)
    return pl.pallas_call(
        functools.partial(_oproj_kernel, part_starts=part_starts),
        grid=(b, tiles_per_batch),
        in_specs=[
            pl.BlockSpec((1, OPROJ_TM, d_na), lambda bi, i: (bi, i, 0)),
            pl.BlockSpec((1, OPROJ_TM, d_dil), lambda bi, i: (bi, i, 0)),
        ] + h_specs + [
            pl.BlockSpec((1, d_na), lambda bi, i: (0, 0)),
            pl.BlockSpec((1, d_dil), lambda bi, i: (0, 0)),
            vec,
            pl.BlockSpec((d, d), lambda bi, i: (0, 0)),
            pl.BlockSpec((1, d), lambda bi, i: (0, 0)), vec, vec,
        ],
        out_specs=[tile, tile],
        out_shape=[jax.ShapeDtypeStruct((b, s, d), F32), jax.ShapeDtypeStruct((b, s, d), BF16)],
        compiler_params=_params(("parallel", "parallel"), 56),
        name="oproj",
    )(o_na, o_dil, *h_parts, g_na.reshape(1, d_na), g_dil.reshape(1, d_dil), gate, w_o,
      gain.reshape(1, d), shift, scale)


def kernel(x, c, w_ada, b_ada, g_ffn1, w1_gate, w1_up, w1_down, g_mix, w_qkv, qn_na, kn_na, qn_dil, kn_dil, rpb_na, g_out_na, g_out_dil, w_o, g_ffn2, w2_gate, w2_up, w2_down):
    b, s, d = x.shape
    depth = w_ada.shape[0]
    rows = s // GRID_W
    rope_cos, rope_sin = _rope_tables(s)
    dil_table = _dil_near_table()
    dil_band = _dil_far_band(s // DIL_FAR)
    c_pad = jnp.pad(c, ((0, 8 - b), (0, 0)))
    lane_order = _rotary_lane_order()
    q_factor = ATTN_SCALE * LOG2_E

    h = x
    for l in range(depth):
        mod = _ada(c_pad, w_ada[l], b_ada[l])[:b]
        sh1, sc1, gt1, sh2, sc2, gt2, sh3, sc3, gt3 = [
            m.reshape(b, 1, d) for m in jnp.split(mod, N_MOD, axis=-1)]

        xn = _norm_mod(h, g_ffn1[l], sh1, sc1)
        n_tiles = b * s // FFN_TM
        h_a, xn_a, w1g, w1u, w1d = _ffn(xn, h, gt1, w1_gate[l], w1_up[l], w1_down[l],
                                        next_norm=(g_mix[l], sh2, sc2),
                                        tiles=(0, 1), cast_weights=True)
        h_b, xn_b = _ffn(xn, h, gt1, w1g, w1u, w1d, next_norm=(g_mix[l], sh2, sc2),
                         tiles=(1, n_tiles - 1))

        head_gains = jnp.stack([
            jnp.stack([qn_na[l] * q_factor, _take_lanes(qn_dil[l], lane_order) * q_factor]),
            jnp.stack([kn_na[l], _take_lanes(kn_dil[l], lane_order)]),
        ]).reshape(2, 2, 1, HEAD_DIM)
        qkv_na, qkv_dil = _qkv((xn_a, xn_b), b, s, _prep_qkv_weight(w_qkv[l]), head_gains,
                               rope_cos, rope_sin)

        o_na, w2g, w2u, wo = _na_attention(qkv_na, _na_bias(rpb_na[l], rows),
                                           side_weights=(w2_gate[l], w2_up[l], w_o[l]))
        o_dil, w2d = _dil_attention(qkv_dil, dil_table, dil_band, side_weights=(w2_down[l],))
        h, xn = _oproj(o_na, o_dil, (h_a, h_b), g_out_na[l], g_out_dil[l], gt2, wo,
                       next_norm=(g_ffn2[l], sh3, sc3))

        h = _ffn(xn, h, gt3, w2g, w2u, w2d).reshape(b, s, d)
    return h
```

```python
import functools
import math

import numpy as np
import jax
import jax.numpy as jnp
from jax import lax
from jax.experimental import pallas as pl
from jax.experimental.pallas import tpu as pltpu

HEAD_DIM = 128
N_HEADS = 16
N_HEADS_NA = 8
N_HEADS_DIL = N_HEADS - N_HEADS_NA
GRID_W = 64
NA_ROWS = 8
NA_COLS = 16
DIL_PATTERNS = ((128, 1), (512, 4), (2048, 16))
ROPE_THETA = 500000.0
ROPE_DIM = HEAD_DIM // 4
N_MOD = 9
EPS = 1e-6
NEG = -1e30
ATTN_SCALE = HEAD_DIM ** -0.5
LOG2_E = math.log2(math.e)

BF16 = jnp.bfloat16
F32 = jnp.float32

MIB = 1024 * 1024
HBM_OUTPUT_VMEM_MIB = 56

ADA_TN = 1024
NORM_TM = 512
NORM_ROWS = 32
NORM_UNROLL = 4
FFN_TM = 1024
FFN_TF = 512
FFN_TF_CAST = 256
FFN_RES = 256
QKV_TM = 512
QKV_TN = 2048
QKV_PIECE = 256
SIDE_CAST_COLS = 512
PREP_ROWS = 512
OPROJ_TM = 512
ATTN_UNROLL = 4
NA_RB = 4
NA_KR = 12
DIL_TQ = 256
DIL_NEAR_PATTERNS = DIL_PATTERNS[:-1]
DIL_FAR = DIL_PATTERNS[-1][1]
DIL_FAR_STEPS = DIL_PATTERNS[-1][0] // 2 // DIL_FAR
DIL_SPAN = max(w // 2 // d * d for w, d in DIL_NEAR_PATTERNS)
DIL_WIN = DIL_TQ + 2 * DIL_SPAN


def _params(sem, vmem_mib):
    return pltpu.CompilerParams(dimension_semantics=sem, vmem_limit_bytes=vmem_mib * MIB)


def _silu(x):
    return x * (1.0 / (1.0 + jnp.exp(-x)))


def _rms(x, gain):
    ms = jnp.mean(x * x, axis=-1, keepdims=True)
    return x * lax.rsqrt(ms + EPS) * gain


def _norm_modulate_rows(x_ref, g_ref, sh_ref, sc_ref, xn_ref):
    n_rows = xn_ref.shape[0]
    gain_scale = g_ref[...] * (1.0 + sc_ref[0])
    shift = sh_ref[0]

    def body(i, carry):
        rows = pl.ds(pl.multiple_of(i * NORM_ROWS, NORM_ROWS), NORM_ROWS)
        xn_ref[rows, :] = (_rms(x_ref[rows, :], gain_scale) + shift).astype(BF16)
        return carry

    lax.fori_loop(0, n_rows // NORM_ROWS, body, 0, unroll=NORM_UNROLL)


def _norm_mod_kernel(x_ref, g_ref, sh_ref, sc_ref, xn_ref):
    _norm_modulate_rows(x_ref.at[0], g_ref, sh_ref, sc_ref, xn_ref.at[0])


def _norm_mod(x, gain, shift, scale):
    b, s, d = x.shape
    tile = pl.BlockSpec((1, NORM_TM, d), lambda bi, i: (bi, i, 0))
    vec = pl.BlockSpec((1, 1, d), lambda bi, i: (bi, 0, 0))
    return pl.pallas_call(
        _norm_mod_kernel,
        grid=(b, s // NORM_TM),
        in_specs=[tile, pl.BlockSpec((1, d), lambda bi, i: (0, 0)), vec, vec],
        out_specs=tile,
        out_shape=jax.ShapeDtypeStruct((b, s, d), BF16),
        compiler_params=_params(("parallel", "parallel"), 24),
        name="norm_mod",
    )(x, gain.reshape(1, d), shift, scale)


def _ada_kernel(c_ref, w_ref, b_ref, o_ref):
    a = _silu(c_ref[...]).astype(BF16)
    w = w_ref[...].astype(BF16)
    o_ref[...] = jnp.dot(a, w, preferred_element_type=F32) + b_ref[...]


def _ada(c_pad, w_ada, b_ada):
    rows, d = c_pad.shape
    n = w_ada.shape[1]
    return pl.pallas_call(
        _ada_kernel,
        grid=(n // ADA_TN,),
        in_specs=[
            pl.BlockSpec((rows, d), lambda j: (0, 0)),
            pl.BlockSpec((d, ADA_TN), lambda j: (0, j)),
            pl.BlockSpec((1, ADA_TN), lambda j: (0, j)),
        ],
        out_specs=pl.BlockSpec((rows, ADA_TN), lambda j: (0, j)),
        out_shape=jax.ShapeDtypeStruct((rows, n), F32),
        compiler_params=_params(("arbitrary",), 24),
        name="ada_mod",
    )(c_pad, w_ada, b_ada.reshape(1, n))


def _ffn_kernel(xn_ref, xres_ref, gt_ref, wg_ref, wu_ref, wd_ref, *rest, emit_next, cast_weights):
    rest = list(rest)
    if emit_next:
        gn_ref, shn_ref, scn_ref = rest[:3]
        rest = rest[3:]
    o_ref = rest.pop(0)
    if emit_next:
        xnn_ref = rest.pop(0)
    j = pl.program_id(1)
    n_res = o_ref.shape[2] // FFN_RES

    @pl.when(j == 0)
    def _():
        o_ref[0] = jnp.zeros(o_ref.shape[1:], F32)

    wg, wu, wd = wg_ref[...], wu_ref[...], wd_ref[...]
    if cast_weights:
        wg, wu, wd = wg.astype(BF16), wu.astype(BF16), wd.astype(BF16)
        for out_ref, w in zip(rest, (wg, wu, wd)):
            out_ref[...] = w

    xn = xn_ref[0]
    g = jnp.dot(xn, wg, preferred_element_type=F32)
    u = jnp.dot(xn, wu, preferred_element_type=F32)
    a = (_silu(g) * u).astype(BF16)
    o_ref[0] += (0.5 * gt_ref[0]) * jnp.dot(a, wd, preferred_element_type=F32)

    @pl.when(j < n_res)
    def _():
        cols = pl.ds(pl.multiple_of(j * FFN_RES, FFN_RES), FFN_RES)
        o_ref[0, :, cols] += xres_ref[0]

    if emit_next:
        @pl.when(j == pl.num_programs(1) - 1)
        def _():
            _norm_modulate_rows(o_ref.at[0], gn_ref, shn_ref, scn_ref, xnn_ref.at[0])


def _ffn(xn, x, gate, w_gate, w_up, w_down, next_norm=None, tiles=None, cast_weights=False):
    b, s, d = x.shape
    dff = w_gate.shape[1]
    per_batch = s // FFN_TM
    first, count = tiles if tiles is not None else (0, b * per_batch)
    tf = FFN_TF_CAST if cast_weights else FFN_TF
    n_steps = dff // tf
    n_res = d // FFN_RES
    assert n_res <= n_steps
    batch_of = lambda t: (first + t) // per_batch
    tile_of = lambda t: (first + t) % per_batch
    rows_mode = dict(pipeline_mode=pl.Buffered(1)) if count == 1 else {}
    vec = pl.BlockSpec((1, 1, d), lambda t, j: (batch_of(t), 0, 0))
    out_tile = pl.BlockSpec((1, FFN_TM, d), lambda t, j: (0, t, 0), **rows_mode)
    w_specs = [
        pl.BlockSpec((d, tf), lambda t, j: (0, j)),
        pl.BlockSpec((d, tf), lambda t, j: (0, j)),
        pl.BlockSpec((tf, d), lambda t, j: (j, 0)),
    ]
    in_specs = [
        pl.BlockSpec((1, FFN_TM, d), lambda t, j: (batch_of(t), tile_of(t), 0), **rows_mode),
        pl.BlockSpec((1, FFN_TM, FFN_RES),
                     lambda t, j: (batch_of(t), tile_of(t), jnp.minimum(j, n_res - 1))),
        vec,
    ] + w_specs
    args = [xn, x, gate, w_gate, w_up, w_down]
    out_specs = [out_tile]
    out_shape = [jax.ShapeDtypeStruct((1, count * FFN_TM, d), F32)]
    if next_norm is not None:
        gain, shift, scale = next_norm
        in_specs += [pl.BlockSpec((1, d), lambda t, j: (0, 0)), vec, vec]
        args += [gain.reshape(1, d), shift, scale]
        out_specs.append(out_tile)
        out_shape.append(jax.ShapeDtypeStruct((1, count * FFN_TM, d), BF16))
    if cast_weights:
        out_specs += w_specs
        out_shape += [jax.ShapeDtypeStruct(w.shape, BF16) for w in (w_gate, w_up, w_down)]
    outs = pl.pallas_call(
        functools.partial(_ffn_kernel, emit_next=next_norm is not None, cast_weights=cast_weights),
        grid=(count, n_steps),
        in_specs=in_specs,
        out_specs=out_specs,
        out_shape=out_shape,
        compiler_params=_params(("parallel", "arbitrary"), 56),
        name="ffn",
    )(*args)
    return outs if len(outs) > 1 else outs[0]


def _part_specs(parts, rows, flat_tile_of):
    specs, starts, start = [], [], 0
    for p in parts:
        n = p.shape[1] // rows
        assert n * rows == p.shape[1]
        specs.append(pl.BlockSpec(
            (1, rows, p.shape[2]),
            lambda *g, start=start, n=n: (0, jnp.clip(flat_tile_of(*g) - start, 0, n - 1), 0)))
        starts.append(start)
        start += n
    return specs, tuple(starts)


def _part_tile(refs, starts, flat_tile):
    x = refs[-1][0]
    for ref, next_start in zip(reversed(refs[:-1]), reversed(starts[1:])):
        x = jnp.where(flat_tile < next_start, ref[0], x)
    return x


def _qkv_kernel(*refs, part_starts):
    xn_refs = refs[:len(part_starts)]
    w_ref, hg_ref, cos_ref, sin_ref, o_na_ref, o_dil_ref = refs[len(part_starts):]
    j = pl.program_id(0)
    flat_tile = pl.program_id(1) * pl.num_programs(2) + pl.program_id(2)
    n_pieces = QKV_TN // QKV_PIECE
    heads_per_piece = QKV_PIECE // HEAD_DIM

    def store(head, y):
        if head < N_HEADS_NA:
            o_na_ref[head] = y.astype(BF16)
        else:
            o_dil_ref[head - N_HEADS_NA] = y

    def normed(r, head):
        return _rms(r, hg_ref[0, head // N_HEADS_NA])

    def normed_rotary(r, head):
        y = normed(r, head)
        return y * cos_ref[...] + pltpu.roll(y, HEAD_DIM // 2, 1) * sin_ref[...]

    def pieces(epilogue_of_head):
        xn = _part_tile(xn_refs, part_starts, flat_tile)
        for p in list(range(n_pieces // 2, n_pieces)) + list(range(n_pieces // 2)):
            r = jnp.dot(xn, w_ref[:, p * QKV_PIECE:(p + 1) * QKV_PIECE],
                        preferred_element_type=F32)
            for hh in range(heads_per_piece):
                head = p * heads_per_piece + hh
                store(head, epilogue_of_head(head)(r[:, hh * HEAD_DIM:(hh + 1) * HEAD_DIM], head))

    @pl.when(j < 2)
    def _():
        pieces(lambda head: normed if head < N_HEADS_NA else normed_rotary)

    @pl.when(j == 2)
    def _():
        pieces(lambda head: (lambda r, _: r))


def _qkv(xn_parts, b, s, w_qkv, head_gains, rope_cos, rope_sin):
    d, n = w_qkv.shape
    assert QKV_TN == N_HEADS * HEAD_DIM and n == 3 * QKV_TN
    tiles_per_batch = s // QKV_TM
    rope = pl.BlockSpec((QKV_TM, HEAD_DIM), lambda j, bi, i: (i, 0))
    xn_specs, part_starts = _part_specs(xn_parts, QKV_TM, lambda j, bi, i: bi * tiles_per_batch + i)
    return pl.pallas_call(
        functools.partial(_qkv_kernel, part_starts=part_starts),
        grid=(n // QKV_TN, b, tiles_per_batch),
        in_specs=xn_specs + [
            pl.BlockSpec((d, QKV_TN), lambda j, bi, i: (0, j)),
            pl.BlockSpec((1, 2, 1, HEAD_DIM), lambda j, bi, i: (jnp.minimum(j, 1), 0, 0, 0)),
            rope, rope,
        ],
        out_specs=[
            pl.BlockSpec((None, N_HEADS_NA, QKV_TM, HEAD_DIM), lambda j, bi, i: (bi, j, i, 0)),
            pl.BlockSpec((None, N_HEADS_DIL, QKV_TM, HEAD_DIM), lambda j, bi, i: (bi, j, i, 0)),
        ],
        out_shape=[
            jax.ShapeDtypeStruct((b, 3 * N_HEADS_NA, s, HEAD_DIM), BF16),
            jax.ShapeDtypeStruct((b, 3 * N_HEADS_DIL, s, HEAD_DIM), F32),
        ],
        compiler_params=_params(("arbitrary", "parallel", "parallel"), 48),
        name="qkv",
    )(*xn_parts, w_qkv, head_gains, rope_cos, rope_sin)


def _rotary_lane_order():
    half = ROPE_DIM // 2
    mid = HEAD_DIM // 2
    order = (list(range(half)) + list(range(ROPE_DIM, ROPE_DIM + mid - half))
             + list(range(half, ROPE_DIM)) + list(range(ROPE_DIM + mid - half, HEAD_DIM)))
    assert sorted(order) == list(range(HEAD_DIM))
    return order


def _take_lanes(a, order):
    runs, start = [], 0
    for i in range(1, len(order) + 1):
        if i == len(order) or order[i] != order[i - 1] + 1:
            runs.append(a[..., order[start]:order[i - 1] + 1])
            start = i
    return jnp.concatenate(runs, axis=-1)


def _rope_tables(s):
    f32 = np.float32
    pos = np.arange(s, dtype=f32)
    inv = np.power(f32(ROPE_THETA), -np.arange(0, ROPE_DIM, 2, dtype=f32) / f32(ROPE_DIM)).astype(f32)
    ang = (pos[:, None] * inv[None, :]).astype(f32)
    cos, sin = np.cos(ang).astype(f32), np.sin(ang).astype(f32)
    half = ROPE_DIM // 2
    gap = HEAD_DIM // 2 - half
    ones, zeros = np.ones((s, gap), f32), np.zeros((s, gap), f32)
    rope_cos = np.concatenate([cos, ones, cos, ones], axis=1)
    rope_sin = np.concatenate([-sin, zeros, sin, zeros], axis=1)
    return jnp.asarray(rope_cos), jnp.asarray(rope_sin)


def _lane_runs(order):
    runs, start = [], 0
    for i in range(1, len(order) + 1):
        if i == len(order) or order[i] != order[i - 1] + 1:
            runs.append((start, order[start], i - start))
            start = i
    return runs


def _prep_qkv_kernel(w_ref, o_ref):
    j = pl.program_id(1)

    @pl.when(j == 2)
    def _():
        o_ref[...] = w_ref[...].astype(BF16)

    @pl.when(j < 2)
    def _():
        lane = lax.broadcasted_iota(jnp.int32, (w_ref.shape[0], HEAD_DIM), 1)
        runs = _lane_runs(_rotary_lane_order())
        for head in range(N_HEADS):
            cols = slice(head * HEAD_DIM, (head + 1) * HEAD_DIM)
            x = w_ref[:, cols]
            if head >= N_HEADS_NA:
                y = x
                for dst, src, length in runs:
                    if dst != src:
                        moved = pltpu.roll(x, (dst - src) % HEAD_DIM, 1)
                        y = jnp.where(jnp.logical_and(lane >= dst, lane < dst + length), moved, y)
                x = y
            o_ref[:, cols] = x.astype(BF16)


def _prep_qkv_weight(w_qkv):
    d, n = w_qkv.shape
    assert n == 3 * QKV_TN
    block = pl.BlockSpec((PREP_ROWS, QKV_TN), lambda i, j: (i, j))
    return pl.pallas_call(
        _prep_qkv_kernel,
        grid=(d // PREP_ROWS, 3),
        in_specs=[block],
        out_specs=block,
        out_shape=jax.ShapeDtypeStruct((d, n), BF16),
        compiler_params=_params(("parallel", "arbitrary"), HBM_OUTPUT_VMEM_MIB),
        name="prep_qkv_weight",
    )(w_qkv)


def _na_window_start(ib, rows):
    return jnp.clip(NA_RB * ib - NA_ROWS // 2, 0, rows - NA_KR)


def _na_bias_kernel(rpb_ref, o_ref, *, rows):
    n_ro = 2 * NA_ROWS - 1
    lanes = 2 * GRID_W
    assert lanes == rpb_ref.shape[2]
    c = lax.broadcasted_iota(jnp.int32, (GRID_W, lanes), 0)
    lane = lax.broadcasted_iota(jnp.int32, (GRID_W, lanes), 1)
    kc = lane & (GRID_W - 1)
    cs = jnp.clip(c - NA_COLS // 2, 0, GRID_W - NA_COLS)
    col_ok = jnp.logical_and(kc >= cs, kc < cs + NA_COLS)
    low_half = lane < GRID_W
    neg = jnp.full((GRID_W, lanes), NEG, F32)
    toeplitz = []
    for ro in range(n_ro):
        row = jnp.broadcast_to(rpb_ref[0, ro:ro + 1, :], (GRID_W, lanes))
        shift = lanes - (NA_COLS - 1)
        lo = pltpu.roll(row, shift, 1, stride=1, stride_axis=0)
        hi = pltpu.roll(row, (shift + GRID_W) % lanes, 1, stride=1, stride_axis=0)
        toeplitz.append(jnp.where(col_ok, jnp.where(low_half, lo, hi) * LOG2_E, neg))
    n_blocks = rows // NA_RB
    for var, ib in enumerate((0, n_blocks // 2, n_blocks - 1)):
        ws = min(max(NA_RB * ib - NA_ROWS // 2, 0), rows - NA_KR)
        for qr in range(NA_RB):
            r = NA_RB * ib + qr
            r_start = min(max(r - NA_ROWS // 2, 0), rows - NA_ROWS)
            for g in range(NA_KR // 2):
                pieces = []
                for kr in (2 * g, 2 * g + 1):
                    ka = ws + kr
                    inside = r_start <= ka < r_start + NA_ROWS
                    pieces.append(toeplitz[ka - r + NA_ROWS - 1] if inside else neg)
                o_ref[0, var, qr * GRID_W:(qr + 1) * GRID_W, g * 2 * GRID_W:(g + 1) * 2 * GRID_W] = (
                    jnp.where(low_half, pieces[0], pieces[1]))


def _na_bias(rpb, rows):
    nh, n_ro, n_co = rpb.shape
    lanes = 2 * GRID_W
    rpb_rows = jnp.pad(rpb, ((0, 0), (0, 0), (0, lanes - n_co)))
    return pl.pallas_call(
        functools.partial(_na_bias_kernel, rows=rows),
        grid=(nh,),
        in_specs=[pl.BlockSpec((1, n_ro, lanes), lambda h: (h, 0, 0))],
        out_specs=pl.BlockSpec((1, 3, NA_RB * GRID_W, NA_KR * GRID_W), lambda h: (h, 0, 0, 0)),
        out_shape=jax.ShapeDtypeStruct((nh, 3, NA_RB * GRID_W, NA_KR * GRID_W), F32),
        compiler_params=_params(("arbitrary",), HBM_OUTPUT_VMEM_MIB),
        name="na_bias",
    )(rpb_rows)


def _softmax_pv(s2, v):
    m = jnp.max(s2, axis=-1, keepdims=True)
    p = jnp.exp2(s2 - m)
    l = jnp.sum(p, axis=-1, keepdims=True)
    return jnp.dot(p.astype(BF16), v, preferred_element_type=F32) / l


def _side_cast(refs):
    n = len(refs) // 2
    for src_ref, dst_ref in zip(refs[:n], refs[n:]):
        for c0 in range(0, src_ref.shape[1], SIDE_CAST_COLS):
            cols = slice(c0, c0 + SIDE_CAST_COLS)
            dst_ref[:, cols] = src_ref[:, cols].astype(BF16)


def _side_cast_plumbing(weights, n_steps, step_of):
    specs, shapes = [], []
    for w in weights:
        slab = w.shape[0] // n_steps
        assert slab * n_steps == w.shape[0] and w.shape[1] % SIDE_CAST_COLS == 0
        specs.append(pl.BlockSpec((slab, w.shape[1]), lambda *g: (step_of(*g), 0)))
        shapes.append(jax.ShapeDtypeStruct(w.shape, BF16))
    return specs, shapes


def _na_kernel(q_ref, k_ref, v_ref, bias_ref, *rest, rows):
    n_side = (len(rest) - 1) // 2
    o_ref = rest[n_side]
    _side_cast(rest[:n_side] + rest[n_side + 1:])
    n_blocks = rows // NA_RB
    tq = NA_RB * GRID_W

    def block(ib):
        start = pl.multiple_of(_na_window_start(ib, rows) * GRID_W, GRID_W)
        variant = jnp.where(ib == 0, 0, jnp.where(ib == n_blocks - 1, 2, 1))
        qrows = pl.ds(pl.multiple_of(ib * tq, tq), tq)
        k = k_ref[0, pl.ds(start, NA_KR * GRID_W), :]
        v = v_ref[0, pl.ds(start, NA_KR * GRID_W), :]
        s = lax.dot_general(q_ref[0, qrows, :], k, (((1,), (1,)), ((), ())),
                            preferred_element_type=F32)
        o_ref[0, qrows, :] = _softmax_pv(s + bias_ref[0, variant], v)

    def body(i, carry):
        for u in range(ATTN_UNROLL):
            block(i * ATTN_UNROLL + u)
        return carry

    lax.fori_loop(0, n_blocks // ATTN_UNROLL, body, 0)


def _na_attention(qkv, bias, side_weights=()):
    b, _, s, _ = qkv.shape
    rows = s // GRID_W
    heads_total = N_HEADS_NA
    tq = NA_RB * GRID_W
    side_specs, side_shapes = _side_cast_plumbing(
        side_weights, b * N_HEADS_NA, lambda bi, h: bi * N_HEADS_NA + h)
    head_slab = lambda first: pl.BlockSpec((None, 1, s, HEAD_DIM), lambda bi, h: (bi, first + h, 0, 0))
    return pl.pallas_call(
        functools.partial(_na_kernel, rows=rows),
        grid=(b, N_HEADS_NA),
        in_specs=[
            head_slab(0), head_slab(heads_total), head_slab(2 * heads_total),
            pl.BlockSpec((1, 3, tq, NA_KR * GRID_W), lambda bi, h: (h, 0, 0, 0)),
        ] + side_specs,
        out_specs=[pl.BlockSpec((1, s, HEAD_DIM), lambda bi, h: (bi, 0, h))] + side_specs,
        out_shape=[jax.ShapeDtypeStruct((b, s, N_HEADS_NA * HEAD_DIM), F32)] + side_shapes,
        compiler_params=_params(("parallel", "arbitrary"), 52),
        name="na_attn",
    )(qkv, qkv, qkv, bias, *side_weights)


def _dil_near_table():
    q = np.arange(DIL_TQ)[:, None]
    x = np.arange(DIL_WIN + 2 * DIL_SPAN)[None, :]
    d = x - 2 * DIL_SPAN - q
    mult = np.zeros(d.shape, np.int64)
    for window, dil in DIL_NEAR_PATTERNS:
        reach = (window // 2 // dil) * dil
        mult += ((d % dil == 0) & (np.abs(d) <= reach)).astype(np.int64)
    table = np.where(mult > 0, np.log2(np.maximum(mult, 1)), NEG)
    return jnp.asarray(table, F32)


def _dil_far_band(length):
    i = np.arange(length)
    inside = np.abs(i[:, None] - i[None, :]) <= DIL_FAR_STEPS
    return jnp.asarray(np.where(inside, 0.0, NEG), F32)


def _dil_kernel(q_ref, k_ref, v_ref, t_ref, band_ref, *rest, seq):
    ofar_ref, lsefar_ref = rest[-2:]
    rest = rest[:-2]
    n_side = (len(rest) - 1) // 2
    o_ref = rest[n_side]
    _side_cast(rest[:n_side] + rest[n_side + 1:])
    length = seq // DIL_FAR
    contract_last = (((1,), (1,)), ((), ()))

    def far_block(r):
        rows = pl.ds(r, length, stride=DIL_FAR)
        q, k, v = (ref[0, rows, :].astype(BF16) for ref in (q_ref, k_ref, v_ref))
        s = lax.dot_general(q, k, contract_last, preferred_element_type=F32) + band_ref[...]
        m = jnp.max(s, axis=-1, keepdims=True)
        p = jnp.exp2(s - m)
        l = jnp.sum(p, axis=-1, keepdims=True)
        o = jnp.dot(p.astype(BF16), v, preferred_element_type=F32) / l
        ofar_ref[rows, :] = o
        lsefar_ref[rows, :] = jnp.broadcast_to(m + jnp.log2(l), o.shape)

    def near_block(ib):
        t0 = ib * DIL_TQ
        ws = jnp.clip(t0 - DIL_SPAN, 0, seq - DIL_WIN)
        off = pl.multiple_of(ws - t0 + 2 * DIL_SPAN, DIL_TQ)
        ws = pl.multiple_of(ws, DIL_TQ)
        qrows = pl.ds(pl.multiple_of(t0, DIL_TQ), DIL_TQ)
        k = k_ref[0, pl.ds(ws, DIL_WIN), :].astype(BF16)
        v = v_ref[0, pl.ds(ws, DIL_WIN), :].astype(BF16)
        s = lax.dot_general(q_ref[0, qrows, :].astype(BF16), k, contract_last,
                            preferred_element_type=F32)
        s = s + t_ref[:, pl.ds(off, DIL_WIN)]
        m = jnp.max(s, axis=-1, keepdims=True)
        p = jnp.exp2(s - m)
        l = jnp.sum(p, axis=-1, keepdims=True)
        acc = jnp.dot(p.astype(BF16), v, preferred_element_type=F32)
        lse_far = lsefar_ref[qrows, :]
        top = jnp.maximum(m, lse_far)
        w_near = jnp.exp2(m - top)
        w_far = jnp.exp2(lse_far - top)
        o_ref[0, qrows, :] = (w_near * acc + w_far * ofar_ref[qrows, :]) / (w_near * l + w_far)

    def unrolled(block):
        def body(i, carry):
            for u in range(ATTN_UNROLL):
                block(i * ATTN_UNROLL + u)
            return carry
        return body

    lax.fori_loop(0, DIL_FAR // ATTN_UNROLL, unrolled(far_block), 0)
    lax.fori_loop(0, seq // DIL_TQ // ATTN_UNROLL, unrolled(near_block), 0)


def _dil_attention(qkv, table, band, side_weights=()):
    b, _, s, _ = qkv.shape
    length = s // DIL_FAR
    assert band.shape == (length, length) and length == DIL_TQ
    side_specs, side_shapes = _side_cast_plumbing(
        side_weights, b * N_HEADS_DIL, lambda bi, h: bi * N_HEADS_DIL + h)
    head_slab = lambda first: pl.BlockSpec((None, 1, s, HEAD_DIM), lambda bi, h: (bi, first + h, 0, 0))
    return pl.pallas_call(
        functools.partial(_dil_kernel, seq=s),
        grid=(b, N_HEADS_DIL),
        in_specs=[
            head_slab(0), head_slab(N_HEADS_DIL), head_slab(2 * N_HEADS_DIL),
            pl.BlockSpec(table.shape, lambda bi, h: (0, 0)),
            pl.BlockSpec(band.shape, lambda bi, h: (0, 0)),
        ] + side_specs,
        out_specs=[pl.BlockSpec((1, s, HEAD_DIM), lambda bi, h: (bi, 0, h))] + side_specs,
        out_shape=[jax.ShapeDtypeStruct((b, s, N_HEADS_DIL * HEAD_DIM), F32)] + side_shapes,
        scratch_shapes=[pltpu.VMEM((s, HEAD_DIM), F32), pltpu.VMEM((s, HEAD_DIM), F32)],
        compiler_params=_params(("parallel", "arbitrary"), 52),
        name="dil_attn",
    )(qkv, qkv, qkv, table, band, *side_weights)


def _oproj_kernel(ona_ref, odil_ref, *refs, part_starts):
    h_refs = refs[:len(part_starts)]
    gna_ref, gdil_ref, gt_ref, w_ref, gn_ref, shn_ref, scn_ref, o_ref, xnn_ref = refs[len(part_starts):]
    flat_tile = pl.program_id(0) * pl.num_programs(1) + pl.program_id(1)
    d_na = ona_ref.shape[2]
    na = _rms(ona_ref[0], gna_ref[...]).astype(BF16)
    nd = _rms(odil_ref[0], gdil_ref[...]).astype(BF16)
    mix = jnp.dot(na, w_ref[:d_na, :], preferred_element_type=F32)
    mix = mix + jnp.dot(nd, w_ref[d_na:, :], preferred_element_type=F32)
    o_ref[0] = _part_tile(h_refs, part_starts, flat_tile) + gt_ref[0] * mix
    _norm_modulate_rows(o_ref.at[0], gn_ref, shn_ref, scn_ref, xnn_ref.at[0])


def _oproj(o_na, o_dil, h_parts, g_na, g_dil, gate, w_o, next_norm):
    b, s, d_na = o_na.shape
    d_dil = o_dil.shape[2]
    d = w_o.shape[1]
    gain, shift, scale = next_norm
    tiles_per_batch = s // OPROJ_TM
    tile = pl.BlockSpec((1, OPROJ_TM, d), lambda bi, i: (bi, i, 0))
    vec = pl.BlockSpec((1, 1, d), lambda bi, i: (bi, 0, 0))
    h_specs, part_starts = _part_specs(h_parts, OPROJ_TM, lambda bi, i: bi * tiles_per_batch + i)
    return pl.pallas_call(
        functools.partial(_oproj_kernel, part_starts=part_starts),
        grid=(b, tiles_per_batch),
        in_specs=[
            pl.BlockSpec((1, OPROJ_TM, d_na), lambda bi, i: (bi, i, 0)),
            pl.BlockSpec((1, OPROJ_TM, d_dil), lambda bi, i: (bi, i, 0)),
        ] + h_specs + [
            pl.BlockSpec((1, d_na), lambda bi, i: (0, 0)),
            pl.BlockSpec((1, d_dil), lambda bi, i: (0, 0)),
            vec,
            pl.BlockSpec((d, d), lambda bi, i: (0, 0)),
            pl.BlockSpec((1, d), lambda bi, i: (0, 0)), vec, vec,
        ],
        out_specs=[tile, tile],
        out_shape=[jax.ShapeDtypeStruct((b, s, d), F32), jax.ShapeDtypeStruct((b, s, d), BF16)],
        compiler_params=_params(("parallel", "parallel"), 56),
        name="oproj",
    )(o_na, o_dil, *h_parts, g_na.reshape(1, d_na), g_dil.reshape(1, d_dil), gate, w_o,
      gain.reshape(1, d), shift, scale)


def kernel(x, c, w_ada, b_ada, g_ffn1, w1_gate, w1_up, w1_down, g_mix, w_qkv, qn_na, kn_na, qn_dil, kn_dil, rpb_na, g_out_na, g_out_dil, w_o, g_ffn2, w2_gate, w2_up, w2_down):
    b, s, d = x.shape
    depth = w_ada.shape[0]
    rows = s // GRID_W
    rope_cos, rope_sin = _rope_tables(s)
    dil_table = _dil_near_table()
    dil_band = _dil_far_band(s // DIL_FAR)
    c_pad = jnp.pad(c, ((0, 8 - b), (0, 0)))
    lane_order = _rotary_lane_order()
    q_factor = ATTN_SCALE * LOG2_E

    h = x
    for l in range(depth):
        mod = _ada(c_pad, w_ada[l], b_ada[l])[:b]
        sh1, sc1, gt1, sh2, sc2, gt2, sh3, sc3, gt3 = [
            m.reshape(b, 1, d) for m in jnp.split(mod, N_MOD, axis=-1)]

        xn = _norm_mod(h, g_ffn1[l], sh1, sc1)
        n_tiles = b * s // FFN_TM
        h_a, xn_a, w1g, w1u, w1d = _ffn(xn, h, gt1, w1_gate[l], w1_up[l], w1_down[l],
                                        next_norm=(g_mix[l], sh2, sc2),
                                        tiles=(0, 1), cast_weights=True)
        h_b, xn_b = _ffn(xn, h, gt1, w1g, w1u, w1d, next_norm=(g_mix[l], sh2, sc2),
                         tiles=(1, n_tiles - 1))

        head_gains = jnp.stack([
            jnp.stack([qn_na[l] * q_factor, _take_lanes(qn_dil[l], lane_order) * q_factor]),
            jnp.stack([kn_na[l], _take_lanes(kn_dil[l], lane_order)]),
        ]).reshape(2, 2, 1, HEAD_DIM)
        qkv_na, qkv_dil = _qkv((xn_a, xn_b), b, s, _prep_qkv_weight(w_qkv[l]), head_gains,
                               rope_cos, rope_sin)

        o_na, w2g, wo = _na_attention(qkv_na, _na_bias(rpb_na[l], rows),
                                      side_weights=(w2_gate[l], w_o[l]))
        o_dil, w2u, w2d = _dil_attention(qkv_dil, dil_table, dil_band,
                                         side_weights=(w2_up[l], w2_down[l]))
        h, xn = _oproj(o_na, o_dil, (h_a, h_b), g_out_na[l], g_out_dil[l], gt2, wo,
                       next_norm=(g_ffn2[l], sh3, sc3))

        h = _ffn(xn, h, gt3, w2g, w2u, w2d).reshape(b, s, d)
    return h
```

```python
import functools
import math

import numpy as np
import jax
import jax.numpy as jnp
from jax import lax
from jax.experimental import pallas as pl
from jax.experimental.pallas import tpu as pltpu

HEAD_DIM = 128
N_HEADS = 16
N_HEADS_NA = 8
N_HEADS_DIL = N_HEADS - N_HEADS_NA
GRID_W = 64
NA_ROWS = 8
NA_COLS = 16
DIL_PATTERNS = ((128, 1), (512, 4), (2048, 16))
ROPE_THETA = 500000.0
ROPE_DIM = HEAD_DIM // 4
N_MOD = 9
EPS = 1e-6
NEG = -1e30
ATTN_SCALE = HEAD_DIM ** -0.5
LOG2_E = math.log2(math.e)

BF16 = jnp.bfloat16
F32 = jnp.float32

MIB = 1024 * 1024
HBM_OUTPUT_VMEM_MIB = 56

ADA_TN = 1024
NORM_TM = 1024
NORM_ROWS = 32
NORM_UNROLL = 4
FFN_TM = 1024
FFN_TF = 512
FFN_TF_CAST = 256
FFN_RES = 256
QKV_TM = 512
QKV_TN = 2048
QKV_PIECE = 256
SIDE_CAST_COLS = 512
PREP_ROWS = 512
OPROJ_TM = 512
ATTN_UNROLL = 4
NA_UNROLL = 8
DIL_FAR_UNROLL = 8
NA_RB = 4
NA_KR = 12
DIL_TQ = 256
DIL_NEAR_PATTERNS = DIL_PATTERNS[:-1]
DIL_FAR = DIL_PATTERNS[-1][1]
DIL_FAR_STEPS = DIL_PATTERNS[-1][0] // 2 // DIL_FAR
DIL_SPAN = max(w // 2 // d * d for w, d in DIL_NEAR_PATTERNS)
DIL_WIN = DIL_TQ + 2 * DIL_SPAN


def _params(sem, vmem_mib):
    return pltpu.CompilerParams(dimension_semantics=sem, vmem_limit_bytes=vmem_mib * MIB)


def _silu(x):
    return x * (1.0 / (1.0 + jnp.exp(-x)))


def _rms(x, gain):
    ms = jnp.mean(x * x, axis=-1, keepdims=True)
    return x * lax.rsqrt(ms + EPS) * gain


def _norm_modulate_rows(x_ref, g_ref, sh_ref, sc_ref, xn_ref):
    n_rows = xn_ref.shape[0]
    gain_scale = g_ref[...] * (1.0 + sc_ref[0])
    shift = sh_ref[0]

    def body(i, carry):
        rows = pl.ds(pl.multiple_of(i * NORM_ROWS, NORM_ROWS), NORM_ROWS)
        xn_ref[rows, :] = (_rms(x_ref[rows, :], gain_scale) + shift).astype(BF16)
        return carry

    lax.fori_loop(0, n_rows // NORM_ROWS, body, 0, unroll=NORM_UNROLL)


def _norm_mod_kernel(x_ref, g_ref, sh_ref, sc_ref, xn_ref):
    _norm_modulate_rows(x_ref.at[0], g_ref, sh_ref, sc_ref, xn_ref.at[0])


def _norm_mod(x, gain, shift, scale):
    b, s, d = x.shape
    tile = pl.BlockSpec((1, NORM_TM, d), lambda bi, i: (bi, i, 0))
    vec = pl.BlockSpec((1, 1, d), lambda bi, i: (bi, 0, 0))
    return pl.pallas_call(
        _norm_mod_kernel,
        grid=(b, s // NORM_TM),
        in_specs=[tile, pl.BlockSpec((1, d), lambda bi, i: (0, 0)), vec, vec],
        out_specs=tile,
        out_shape=jax.ShapeDtypeStruct((b, s, d), BF16),
        compiler_params=_params(("parallel", "parallel"), 40),
        name="norm_mod",
    )(x, gain.reshape(1, d), shift, scale)


def _ada_kernel(c_ref, w_ref, b_ref, o_ref):
    a = _silu(c_ref[...]).astype(BF16)
    w = w_ref[...].astype(BF16)
    o_ref[...] = jnp.dot(a, w, preferred_element_type=F32) + b_ref[...]


def _ada(c_pad, w_ada, b_ada):
    rows, d = c_pad.shape
    n = w_ada.shape[1]
    return pl.pallas_call(
        _ada_kernel,
        grid=(n // ADA_TN,),
        in_specs=[
            pl.BlockSpec((rows, d), lambda j: (0, 0)),
            pl.BlockSpec((d, ADA_TN), lambda j: (0, j)),
            pl.BlockSpec((1, ADA_TN), lambda j: (0, j)),
        ],
        out_specs=pl.BlockSpec((rows, ADA_TN), lambda j: (0, j)),
        out_shape=jax.ShapeDtypeStruct((rows, n), F32),
        compiler_params=_params(("arbitrary",), 24),
        name="ada_mod",
    )(c_pad, w_ada, b_ada.reshape(1, n))


def _ffn_kernel(xn_ref, xres_ref, gt_ref, wg_ref, wu_ref, wd_ref, *rest, emit_next, cast_weights):
    rest = list(rest)
    if emit_next:
        gn_ref, shn_ref, scn_ref = rest[:3]
        rest = rest[3:]
    o_ref = rest.pop(0)
    if emit_next:
        xnn_ref = rest.pop(0)
    j = pl.program_id(1)
    n_res = o_ref.shape[2] // FFN_RES

    def contribution():
        wg, wu, wd = wg_ref[...], wu_ref[...], wd_ref[...]
        if cast_weights:
            wg, wu, wd = wg.astype(BF16), wu.astype(BF16), wd.astype(BF16)
            for out_ref, w in zip(rest, (wg, wu, wd)):
                out_ref[...] = w
        xn = xn_ref[0]
        g = jnp.dot(xn, wg, preferred_element_type=F32)
        u = jnp.dot(xn, wu, preferred_element_type=F32)
        a = (_silu(g) * u).astype(BF16)
        half_gate = 0.5 * gt_ref[0]
        xres = xres_ref[0]
        pieces = []
        for c in range(n_res):
            cols = slice(c * FFN_RES, (c + 1) * FFN_RES)
            piece = half_gate[:, cols] * jnp.dot(a, wd[:, cols], preferred_element_type=F32)
            pieces.append((cols, piece + jnp.where(j == c, xres, 0.0)))
        return pieces

    @pl.when(j == 0)
    def _():
        for cols, piece in contribution():
            o_ref[0, :, cols] = piece

    @pl.when(j > 0)
    def _():
        for cols, piece in contribution():
            o_ref[0, :, cols] += piece

    if emit_next:
        @pl.when(j == pl.num_programs(1) - 1)
        def _():
            _norm_modulate_rows(o_ref.at[0], gn_ref, shn_ref, scn_ref, xnn_ref.at[0])


def _ffn(xn, x, gate, w_gate, w_up, w_down, next_norm=None, tiles=None, cast_weights=False):
    b, s, d = x.shape
    dff = w_gate.shape[1]
    per_batch = s // FFN_TM
    first, count = tiles if tiles is not None else (0, b * per_batch)
    tf = FFN_TF_CAST if cast_weights else FFN_TF
    n_steps = dff // tf
    n_res = d // FFN_RES
    assert n_res <= n_steps
    batch_of = lambda t: (first + t) // per_batch
    tile_of = lambda t: (first + t) % per_batch
    rows_mode = dict(pipeline_mode=pl.Buffered(1)) if count == 1 else {}
    vec = pl.BlockSpec((1, 1, d), lambda t, j: (batch_of(t), 0, 0))
    out_tile = pl.BlockSpec((1, FFN_TM, d), lambda t, j: (0, t, 0), **rows_mode)
    w_specs = [
        pl.BlockSpec((d, tf), lambda t, j: (0, j)),
        pl.BlockSpec((d, tf), lambda t, j: (0, j)),
        pl.BlockSpec((tf, d), lambda t, j: (j, 0)),
    ]
    in_specs = [
        pl.BlockSpec((1, FFN_TM, d), lambda t, j: (batch_of(t), tile_of(t), 0), **rows_mode),
        pl.BlockSpec((1, FFN_TM, FFN_RES),
                     lambda t, j: (batch_of(t), tile_of(t), jnp.minimum(j, n_res - 1))),
        vec,
    ] + w_specs
    args = [xn, x, gate, w_gate, w_up, w_down]
    out_specs = [out_tile]
    out_shape = [jax.ShapeDtypeStruct((1, count * FFN_TM, d), F32)]
    if next_norm is not None:
        gain, shift, scale = next_norm
        in_specs += [pl.BlockSpec((1, d), lambda t, j: (0, 0)), vec, vec]
        args += [gain.reshape(1, d), shift, scale]
        out_specs.append(out_tile)
        out_shape.append(jax.ShapeDtypeStruct((1, count * FFN_TM, d), BF16))
    if cast_weights:
        out_specs += w_specs
        out_shape += [jax.ShapeDtypeStruct(w.shape, BF16) for w in (w_gate, w_up, w_down)]
    outs = pl.pallas_call(
        functools.partial(_ffn_kernel, emit_next=next_norm is not None, cast_weights=cast_weights),
        grid=(count, n_steps),
        in_specs=in_specs,
        out_specs=out_specs,
        out_shape=out_shape,
        compiler_params=_params(("parallel", "arbitrary"), 56),
        name="ffn",
    )(*args)
    return outs if len(outs) > 1 else outs[0]


def _part_specs(parts, rows, flat_tile_of):
    specs, starts, start = [], [], 0
    for p in parts:
        n = p.shape[1] // rows
        assert n * rows == p.shape[1]
        specs.append(pl.BlockSpec(
            (1, rows, p.shape[2]),
            lambda *g, start=start, n=n: (0, jnp.clip(flat_tile_of(*g) - start, 0, n - 1), 0)))
        starts.append(start)
        start += n
    return specs, tuple(starts)


def _part_tile(refs, starts, flat_tile):
    x = refs[-1][0]
    for ref, next_start in zip(reversed(refs[:-1]), reversed(starts[1:])):
        x = jnp.where(flat_tile < next_start, ref[0], x)
    return x


def _qkv_kernel(*refs, part_starts):
    xn_refs = refs[:len(part_starts)]
    w_ref, hg_ref, cos_ref, sin_ref, o_na_ref, o_dil_ref = refs[len(part_starts):]
    j = pl.program_id(0)
    flat_tile = pl.program_id(1) * pl.num_programs(2) + pl.program_id(2)
    n_pieces = QKV_TN // QKV_PIECE
    heads_per_piece = QKV_PIECE // HEAD_DIM

    def store(head, y):
        if head < N_HEADS_NA:
            o_na_ref[head] = y.astype(BF16)
        else:
            o_dil_ref[head - N_HEADS_NA] = y

    def normed(r, head):
        return _rms(r, hg_ref[0, head // N_HEADS_NA])

    def normed_rotary(r, head):
        y = normed(r, head)
        return y * cos_ref[...] + pltpu.roll(y, HEAD_DIM // 2, 1) * sin_ref[...]

    def pieces(epilogue_of_head):
        xn = _part_tile(xn_refs, part_starts, flat_tile)
        for p in list(range(n_pieces // 2, n_pieces)) + list(range(n_pieces // 2)):
            r = jnp.dot(xn, w_ref[:, p * QKV_PIECE:(p + 1) * QKV_PIECE],
                        preferred_element_type=F32)
            for hh in range(heads_per_piece):
                head = p * heads_per_piece + hh
                store(head, epilogue_of_head(head)(r[:, hh * HEAD_DIM:(hh + 1) * HEAD_DIM], head))

    @pl.when(j < 2)
    def _():
        pieces(lambda head: normed if head < N_HEADS_NA else normed_rotary)

    @pl.when(j == 2)
    def _():
        pieces(lambda head: (lambda r, _: r))


def _qkv(xn_parts, b, s, w_qkv, head_gains, rope_cos, rope_sin):
    d, n = w_qkv.shape
    assert QKV_TN == N_HEADS * HEAD_DIM and n == 3 * QKV_TN
    tiles_per_batch = s // QKV_TM
    rope = pl.BlockSpec((QKV_TM, HEAD_DIM), lambda j, bi, i: (i, 0))
    xn_specs, part_starts = _part_specs(xn_parts, QKV_TM, lambda j, bi, i: bi * tiles_per_batch + i)
    return pl.pallas_call(
        functools.partial(_qkv_kernel, part_starts=part_starts),
        grid=(n // QKV_TN, b, tiles_per_batch),
        in_specs=xn_specs + [
            pl.BlockSpec((d, QKV_TN), lambda j, bi, i: (0, j)),
            pl.BlockSpec((1, 2, 1, HEAD_DIM), lambda j, bi, i: (jnp.minimum(j, 1), 0, 0, 0)),
            rope, rope,
        ],
        out_specs=[
            pl.BlockSpec((None, N_HEADS_NA, QKV_TM, HEAD_DIM), lambda j, bi, i: (bi, j, i, 0)),
            pl.BlockSpec((None, N_HEADS_DIL, QKV_TM, HEAD_DIM), lambda j, bi, i: (bi, j, i, 0)),
        ],
        out_shape=[
            jax.ShapeDtypeStruct((b, 3 * N_HEADS_NA, s, HEAD_DIM), BF16),
            jax.ShapeDtypeStruct((b, 3 * N_HEADS_DIL, s, HEAD_DIM), F32),
        ],
        compiler_params=_params(("parallel", "parallel", "parallel"), 48),
        name="qkv",
    )(*xn_parts, w_qkv, head_gains, rope_cos, rope_sin)


def _rotary_lane_order():
    half = ROPE_DIM // 2
    mid = HEAD_DIM // 2
    order = (list(range(half)) + list(range(ROPE_DIM, ROPE_DIM + mid - half))
             + list(range(half, ROPE_DIM)) + list(range(ROPE_DIM + mid - half, HEAD_DIM)))
    assert sorted(order) == list(range(HEAD_DIM))
    return order


def _take_lanes(a, order):
    runs, start = [], 0
    for i in range(1, len(order) + 1):
        if i == len(order) or order[i] != order[i - 1] + 1:
            runs.append(a[..., order[start]:order[i - 1] + 1])
            start = i
    return jnp.concatenate(runs, axis=-1)


def _rope_tables(s):
    f32 = np.float32
    pos = np.arange(s, dtype=f32)
    inv = np.power(f32(ROPE_THETA), -np.arange(0, ROPE_DIM, 2, dtype=f32) / f32(ROPE_DIM)).astype(f32)
    ang = (pos[:, None] * inv[None, :]).astype(f32)
    cos, sin = np.cos(ang).astype(f32), np.sin(ang).astype(f32)
    half = ROPE_DIM // 2
    gap = HEAD_DIM // 2 - half
    ones, zeros = np.ones((s, gap), f32), np.zeros((s, gap), f32)
    rope_cos = np.concatenate([cos, ones, cos, ones], axis=1)
    rope_sin = np.concatenate([-sin, zeros, sin, zeros], axis=1)
    return jnp.asarray(rope_cos), jnp.asarray(rope_sin)


def _lane_runs(order):
    runs, start = [], 0
    for i in range(1, len(order) + 1):
        if i == len(order) or order[i] != order[i - 1] + 1:
            runs.append((start, order[start], i - start))
            start = i
    return runs


def _prep_qkv_kernel(w_ref, o_ref):
    j = pl.program_id(1)

    @pl.when(j == 2)
    def _():
        o_ref[...] = w_ref[...].astype(BF16)

    @pl.when(j < 2)
    def _():
        lane = lax.broadcasted_iota(jnp.int32, (w_ref.shape[0], HEAD_DIM), 1)
        runs = _lane_runs(_rotary_lane_order())
        for head in range(N_HEADS):
            cols = slice(head * HEAD_DIM, (head + 1) * HEAD_DIM)
            x = w_ref[:, cols]
            if head >= N_HEADS_NA:
                y = x
                for dst, src, length in runs:
                    if dst != src:
                        moved = pltpu.roll(x, (dst - src) % HEAD_DIM, 1)
                        y = jnp.where(jnp.logical_and(lane >= dst, lane < dst + length), moved, y)
                x = y
            o_ref[:, cols] = x.astype(BF16)


def _prep_qkv_weight(w_qkv):
    d, n = w_qkv.shape
    assert n == 3 * QKV_TN
    block = pl.BlockSpec((PREP_ROWS, QKV_TN), lambda i, j: (i, j))
    return pl.pallas_call(
        _prep_qkv_kernel,
        grid=(d // PREP_ROWS, 3),
        in_specs=[block],
        out_specs=block,
        out_shape=jax.ShapeDtypeStruct((d, n), BF16),
        compiler_params=_params(("parallel", "arbitrary"), HBM_OUTPUT_VMEM_MIB),
        name="prep_qkv_weight",
    )(w_qkv)


def _na_window_start(ib, rows):
    return jnp.clip(NA_RB * ib - NA_ROWS // 2, 0, rows - NA_KR)


def _na_bias_kernel(rpb_ref, o_ref, *, rows):
    n_ro = 2 * NA_ROWS - 1
    lanes = 2 * GRID_W
    assert lanes == rpb_ref.shape[2]
    c = lax.broadcasted_iota(jnp.int32, (GRID_W, lanes), 0)
    lane = lax.broadcasted_iota(jnp.int32, (GRID_W, lanes), 1)
    kc = lane & (GRID_W - 1)
    cs = jnp.clip(c - NA_COLS // 2, 0, GRID_W - NA_COLS)
    col_ok = jnp.logical_and(kc >= cs, kc < cs + NA_COLS)
    low_half = lane < GRID_W
    neg = jnp.full((GRID_W, lanes), NEG, F32)
    toeplitz = []
    for ro in range(n_ro):
        row = jnp.broadcast_to(rpb_ref[0, ro:ro + 1, :], (GRID_W, lanes))
        shift = lanes - (NA_COLS - 1)
        lo = pltpu.roll(row, shift, 1, stride=1, stride_axis=0)
        hi = pltpu.roll(row, (shift + GRID_W) % lanes, 1, stride=1, stride_axis=0)
        toeplitz.append(jnp.where(col_ok, jnp.where(low_half, lo, hi) * LOG2_E, neg))
    n_blocks = rows // NA_RB
    for var, ib in enumerate((0, n_blocks // 2, n_blocks - 1)):
        ws = min(max(NA_RB * ib - NA_ROWS // 2, 0), rows - NA_KR)
        for qr in range(NA_RB):
            r = NA_RB * ib + qr
            r_start = min(max(r - NA_ROWS // 2, 0), rows - NA_ROWS)
            for g in range(NA_KR // 2):
                pieces = []
                for kr in (2 * g, 2 * g + 1):
                    ka = ws + kr
                    inside = r_start <= ka < r_start + NA_ROWS
                    pieces.append(toeplitz[ka - r + NA_ROWS - 1] if inside else neg)
                o_ref[0, var, qr * GRID_W:(qr + 1) * GRID_W, g * 2 * GRID_W:(g + 1) * 2 * GRID_W] = (
                    jnp.where(low_half, pieces[0], pieces[1]))


def _na_bias(rpb, rows):
    nh, n_ro, n_co = rpb.shape
    lanes = 2 * GRID_W
    rpb_rows = jnp.pad(rpb, ((0, 0), (0, 0), (0, lanes - n_co)))
    return pl.pallas_call(
        functools.partial(_na_bias_kernel, rows=rows),
        grid=(nh,),
        in_specs=[pl.BlockSpec((1, n_ro, lanes), lambda h: (h, 0, 0))],
        out_specs=pl.BlockSpec((1, 3, NA_RB * GRID_W, NA_KR * GRID_W), lambda h: (h, 0, 0, 0)),
        out_shape=jax.ShapeDtypeStruct((nh, 3, NA_RB * GRID_W, NA_KR * GRID_W), F32),
        compiler_params=_params(("arbitrary",), HBM_OUTPUT_VMEM_MIB),
        name="na_bias",
    )(rpb_rows)


def _softmax_pv(s2, v):
    m = jnp.max(s2, axis=-1, keepdims=True)
    p = jnp.exp2(s2 - m)
    l = jnp.sum(p, axis=-1, keepdims=True)
    return jnp.dot(p.astype(BF16), v, preferred_element_type=F32) / l


def _side_cast(refs):
    n = len(refs) // 2
    for src_ref, dst_ref in zip(refs[:n], refs[n:]):
        for c0 in range(0, src_ref.shape[1], SIDE_CAST_COLS):
            cols = slice(c0, c0 + SIDE_CAST_COLS)
            dst_ref[:, cols] = src_ref[:, cols].astype(BF16)


def _side_cast_plumbing(weights, n_steps, step_of):
    specs, shapes = [], []
    for w in weights:
        slab = w.shape[0] // n_steps
        assert slab * n_steps == w.shape[0] and w.shape[1] % SIDE_CAST_COLS == 0
        specs.append(pl.BlockSpec((slab, w.shape[1]), lambda *g: (step_of(*g), 0)))
        shapes.append(jax.ShapeDtypeStruct(w.shape, BF16))
    return specs, shapes


def _na_kernel(q_ref, k_ref, v_ref, bias_ref, *rest, rows):
    n_side = (len(rest) - 1) // 2
    o_ref = rest[n_side]
    _side_cast(rest[:n_side] + rest[n_side + 1:])
    n_blocks = rows // NA_RB
    tq = NA_RB * GRID_W

    def block(ib):
        start = pl.multiple_of(_na_window_start(ib, rows) * GRID_W, GRID_W)
        variant = jnp.where(ib == 0, 0, jnp.where(ib == n_blocks - 1, 2, 1))
        qrows = pl.ds(pl.multiple_of(ib * tq, tq), tq)
        k = k_ref[0, pl.ds(start, NA_KR * GRID_W), :]
        v = v_ref[0, pl.ds(start, NA_KR * GRID_W), :]
        s = lax.dot_general(q_ref[0, qrows, :], k, (((1,), (1,)), ((), ())),
                            preferred_element_type=F32)
        o_ref[0, qrows, :] = _softmax_pv(s + bias_ref[0, variant], v)

    def body(i, carry):
        for u in range(NA_UNROLL):
            block(i * NA_UNROLL + u)
        return carry

    lax.fori_loop(0, n_blocks // NA_UNROLL, body, 0)


def _na_attention(qkv, bias, side_weights=()):
    b, _, s, _ = qkv.shape
    rows = s // GRID_W
    heads_total = N_HEADS_NA
    tq = NA_RB * GRID_W
    side_specs, side_shapes = _side_cast_plumbing(
        side_weights, b * N_HEADS_NA, lambda bi, h: bi * N_HEADS_NA + h)
    head_slab = lambda first: pl.BlockSpec((None, 1, s, HEAD_DIM), lambda bi, h: (bi, first + h, 0, 0))
    return pl.pallas_call(
        functools.partial(_na_kernel, rows=rows),
        grid=(b, N_HEADS_NA),
        in_specs=[
            head_slab(0), head_slab(heads_total), head_slab(2 * heads_total),
            pl.BlockSpec((1, 3, tq, NA_KR * GRID_W), lambda bi, h: (h, 0, 0, 0)),
        ] + side_specs,
        out_specs=[pl.BlockSpec((1, s, HEAD_DIM), lambda bi, h: (bi, 0, h))] + side_specs,
        out_shape=[jax.ShapeDtypeStruct((b, s, N_HEADS_NA * HEAD_DIM), F32)] + side_shapes,
        compiler_params=_params(("parallel", "arbitrary"), 52),
        name="na_attn",
    )(qkv, qkv, qkv, bias, *side_weights)


def _dil_near_table():
    q = np.arange(DIL_TQ)[:, None]
    x = np.arange(DIL_WIN + 2 * DIL_SPAN)[None, :]
    d = x - 2 * DIL_SPAN - q
    mult = np.zeros(d.shape, np.int64)
    for window, dil in DIL_NEAR_PATTERNS:
        reach = (window // 2 // dil) * dil
        mult += ((d % dil == 0) & (np.abs(d) <= reach)).astype(np.int64)
    table = np.where(mult > 0, np.log2(np.maximum(mult, 1)), NEG)
    return jnp.asarray(table, F32)


def _dil_far_band(length):
    i = np.arange(length)
    inside = np.abs(i[:, None] - i[None, :]) <= DIL_FAR_STEPS
    return jnp.asarray(np.where(inside, 0.0, NEG), F32)


def _dil_kernel(q_ref, k_ref, v_ref, t_ref, band_ref, *rest, seq):
    ofar_ref, lsefar_ref = rest[-2:]
    rest = rest[:-2]
    n_side = (len(rest) - 1) // 2
    o_ref = rest[n_side]
    _side_cast(rest[:n_side] + rest[n_side + 1:])
    length = seq // DIL_FAR
    contract_last = (((1,), (1,)), ((), ()))

    def far_block(r):
        rows = pl.ds(r, length, stride=DIL_FAR)
        q, k, v = (ref[0, rows, :].astype(BF16) for ref in (q_ref, k_ref, v_ref))
        s = lax.dot_general(q, k, contract_last, preferred_element_type=F32) + band_ref[...]
        m = jnp.max(s, axis=-1, keepdims=True)
        p = jnp.exp2(s - m)
        l = jnp.sum(p, axis=-1, keepdims=True)
        o = jnp.dot(p.astype(BF16), v, preferred_element_type=F32) / l
        ofar_ref[rows, :] = o
        lsefar_ref[rows, :] = jnp.broadcast_to(m + jnp.log2(l), o.shape)

    def near_block(ib):
        t0 = ib * DIL_TQ
        ws = jnp.clip(t0 - DIL_SPAN, 0, seq - DIL_WIN)
        off = pl.multiple_of(ws - t0 + 2 * DIL_SPAN, DIL_TQ)
        ws = pl.multiple_of(ws, DIL_TQ)
        qrows = pl.ds(pl.multiple_of(t0, DIL_TQ), DIL_TQ)
        k = k_ref[0, pl.ds(ws, DIL_WIN), :].astype(BF16)
        v = v_ref[0, pl.ds(ws, DIL_WIN), :].astype(BF16)
        s = lax.dot_general(q_ref[0, qrows, :].astype(BF16), k, contract_last,
                            preferred_element_type=F32)
        s = s + t_ref[:, pl.ds(off, DIL_WIN)]
        m = jnp.max(s, axis=-1, keepdims=True)
        p = jnp.exp2(s - m)
        l = jnp.sum(p, axis=-1, keepdims=True)
        acc = jnp.dot(p.astype(BF16), v, preferred_element_type=F32)
        lse_far = lsefar_ref[qrows, :]
        top = jnp.maximum(m, lse_far)
        w_near = jnp.exp2(m - top)
        w_far = jnp.exp2(lse_far - top)
        o_ref[0, qrows, :] = (w_near * acc + w_far * ofar_ref[qrows, :]) / (w_near * l + w_far)

    def unrolled(block, factor):
        def body(i, carry):
            for u in range(factor):
                block(i * factor + u)
            return carry
        return body

    lax.fori_loop(0, DIL_FAR // DIL_FAR_UNROLL, unrolled(far_block, DIL_FAR_UNROLL), 0)
    lax.fori_loop(0, seq // DIL_TQ // ATTN_UNROLL, unrolled(near_block, ATTN_UNROLL), 0)


def _dil_attention(qkv, table, band, side_weights=()):
    b, _, s, _ = qkv.shape
    length = s // DIL_FAR
    assert band.shape == (length, length) and length == DIL_TQ
    side_specs, side_shapes = _side_cast_plumbing(
        side_weights, b * N_HEADS_DIL, lambda bi, h: bi * N_HEADS_DIL + h)
    head_slab = lambda first: pl.BlockSpec((None, 1, s, HEAD_DIM), lambda bi, h: (bi, first + h, 0, 0))
    return pl.pallas_call(
        functools.partial(_dil_kernel, seq=s),
        grid=(b, N_HEADS_DIL),
        in_specs=[
            head_slab(0), head_slab(N_HEADS_DIL), head_slab(2 * N_HEADS_DIL),
            pl.BlockSpec(table.shape, lambda bi, h: (0, 0)),
            pl.BlockSpec(band.shape, lambda bi, h: (0, 0)),
        ] + side_specs,
        out_specs=[pl.BlockSpec((1, s, HEAD_DIM), lambda bi, h: (bi, 0, h))] + side_specs,
        out_shape=[jax.ShapeDtypeStruct((b, s, N_HEADS_DIL * HEAD_DIM), F32)] + side_shapes,
        scratch_shapes=[pltpu.VMEM((s, HEAD_DIM), F32), pltpu.VMEM((s, HEAD_DIM), F32)],
        compiler_params=_params(("parallel", "arbitrary"), 52),
        name="dil_attn",
    )(qkv, qkv, qkv, table, band, *side_weights)


def _oproj_kernel(ona_ref, odil_ref, *refs, part_starts):
    h_refs = refs[:len(part_starts)]
    gna_ref, gdil_ref, gt_ref, w_ref, gn_ref, shn_ref, scn_ref, o_ref, xnn_ref = refs[len(part_starts):]
    flat_tile = pl.program_id(0) * pl.num_programs(1) + pl.program_id(1)
    d_na = ona_ref.shape[2]
    na = _rms(ona_ref[0], gna_ref[...]).astype(BF16)
    nd = _rms(odil_ref[0], gdil_ref[...]).astype(BF16)
    mix = jnp.dot(na, w_ref[:d_na, :], preferred_element_type=F32)
    mix = mix + jnp.dot(nd, w_ref[d_na:, :], preferred_element_type=F32)
    o_ref[0] = _part_tile(h_refs, part_starts, flat_tile) + gt_ref[0] * mix
    _norm_modulate_rows(o_ref.at[0], gn_ref, shn_ref, scn_ref, xnn_ref.at[0])


def _oproj(o_na, o_dil, h_parts, g_na, g_dil, gate, w_o, next_norm):
    b, s, d_na = o_na.shape
    d_dil = o_dil.shape[2]
    d = w_o.shape[1]
    gain, shift, scale = next_norm
    tiles_per_batch = s // OPROJ_TM
    tile = pl.BlockSpec((1, OPROJ_TM, d), lambda bi, i: (bi, i, 0))
    vec = pl.BlockSpec((1, 1, d), lambda bi, i: (bi, 0, 0))
    h_specs, part_starts = _part_specs(h_parts, OPROJ_TM, lambda bi, i: bi * tiles_per_batch + i)
    return pl.pallas_call(
        functools.partial(_oproj_kernel, part_starts=part_starts),
        grid=(b, tiles_per_batch),
        in_specs=[
            pl.BlockSpec((1, OPROJ_TM, d_na), lambda bi, i: (bi, i, 0)),
            pl.BlockSpec((1, OPROJ_TM, d_dil), lambda bi, i: (bi, i, 0)),
        ] + h_specs + [
            pl.BlockSpec((1, d_na), lambda bi, i: (0, 0)),
            pl.BlockSpec((1, d_dil), lambda bi, i: (0, 0)),
            vec,
            pl.BlockSpec((d, d), lambda bi, i: (0, 0)),
            pl.BlockSpec((1, d), lambda bi, i: (0, 0)), vec, vec,
        ],
        out_specs=[tile, tile],
        out_shape=[jax.ShapeDtypeStruct((b, s, d), F32), jax.ShapeDtypeStruct((b, s, d), BF16)],
        compiler_params=_params(("parallel", "parallel"), 56),
        name="oproj",
    )(o_na, o_dil, *h_parts, g_na.reshape(1, d_na), g_dil.reshape(1, d_dil), gate, w_o,
      gain.reshape(1, d), shift, scale)


def kernel(x, c, w_ada, b_ada, g_ffn1, w1_gate, w1_up, w1_down, g_mix, w_qkv, qn_na, kn_na, qn_dil, kn_dil, rpb_na, g_out_na, g_out_dil, w_o, g_ffn2, w2_gate, w2_up, w2_down):
    b, s, d = x.shape
    depth = w_ada.shape[0]
    rows = s // GRID_W
    rope_cos, rope_sin = _rope_tables(s)
    dil_table = _dil_near_table()
    dil_band = _dil_far_band(s // DIL_FAR)
    c_pad = jnp.pad(c, ((0, 8 - b), (0, 0)))
    lane_order = _rotary_lane_order()
    q_factor = ATTN_SCALE * LOG2_E

    h = x
    for l in range(depth):
        mod = _ada(c_pad, w_ada[l], b_ada[l])[:b]
        sh1, sc1, gt1, sh2, sc2, gt2, sh3, sc3, gt3 = [
            m.reshape(b, 1, d) for m in jnp.split(mod, N_MOD, axis=-1)]

        xn = _norm_mod(h, g_ffn1[l], sh1, sc1)
        n_tiles = b * s // FFN_TM
        h_a, xn_a, w1g, w1u, w1d = _ffn(xn, h, gt1, w1_gate[l], w1_up[l], w1_down[l],
                                        next_norm=(g_mix[l], sh2, sc2),
                                        tiles=(0, 1), cast_weights=True)
        h_b, xn_b = _ffn(xn, h, gt1, w1g, w1u, w1d, next_norm=(g_mix[l], sh2, sc2),
                         tiles=(1, n_tiles - 1))

        head_gains = jnp.stack([
            jnp.stack([qn_na[l] * q_factor, _take_lanes(qn_dil[l], lane_order) * q_factor]),
            jnp.stack([kn_na[l], _take_lanes(kn_dil[l], lane_order)]),
        ]).reshape(2, 2, 1, HEAD_DIM)
        qkv_na, qkv_dil = _qkv((xn_a, xn_b), b, s, _prep_qkv_weight(w_qkv[l]), head_gains,
                               rope_cos, rope_sin)

        o_na, w2g, wo = _na_attention(qkv_na, _na_bias(rpb_na[l], rows),
                                      side_weights=(w2_gate[l], w_o[l]))
        o_dil, w2u, w2d = _dil_attention(qkv_dil, dil_table, dil_band,
                                         side_weights=(w2_up[l], w2_down[l]))
        h, xn = _oproj(o_na, o_dil, (h_a, h_b), g_out_na[l], g_out_dil[l], gt2, wo,
                       next_norm=(g_ffn2[l], sh3, sc3))

        h = _ffn(xn, h, gt3, w2g, w2u, w2d).reshape(b, s, d)
    return h
```
